```python
import math
import jax, jax.numpy as jnp
from jax import lax
import numpy as np

D_MODEL = 1024
BATCH = 4
SEQ = 4096
DEPTH = 2
DEC_BATCH = 8
DEC_SEQ = 8192
PAST_LEN = 128

GRID_W = 64
NA_HEADS = 8
NA_HEAD_DIM = 64
NA_WIDTH = NA_HEADS * NA_HEAD_DIM
WIN_R = 8
WIN_C = 16
Q_BLOCK_C = 16
K_BLOCK_C = 32
DN_HEADS = 4
DN_HEAD_DIM = 128
DN_WIDTH = DN_HEADS * DN_HEAD_DIM
CONV_K = 5
CHUNK = 64
N_DIR = 2
EPS = 1e-6
PROJ_SIZES = (3 * NA_WIDTH, NA_WIDTH, 3 * DN_WIDTH, DN_WIDTH, N_DIR * DN_HEADS, N_DIR * DN_HEADS, D_MODEL, D_MODEL)
D_IN = 3 * NA_WIDTH + NA_WIDTH + 3 * DN_WIDTH + DN_WIDTH + 2 * N_DIR * DN_HEADS + 2 * D_MODEL

kernel_name = "hybrid_natten_gdn_encoder"


def _rms_norm(x, g):
    xf = x.astype(jnp.float32)
    y = xf * lax.rsqrt(jnp.mean(xf * xf, axis=-1, keepdims=True) + EPS)
    return (y * g.astype(jnp.float32)).astype(x.dtype)


def _l2_norm(x):
    return x * lax.rsqrt(jnp.sum(x * x, axis=-1, keepdims=True) + EPS)


def _neighbourhood_attention(q, k, v, rpb):
    B, T, H, d = q.shape
    rows = T // GRID_W
    wr = min(WIN_R, rows)
    n_cb = GRID_W // Q_BLOCK_C
    qcol = jnp.arange(GRID_W).reshape(n_cb, Q_BLOCK_C)
    col_start = jnp.clip(qcol - WIN_C // 2, 0, GRID_W - WIN_C)
    kcol = jnp.clip(jnp.arange(n_cb) * Q_BLOCK_C - WIN_C // 2, 0, GRID_W - K_BLOCK_C)[:, None] \
        + jnp.arange(K_BLOCK_C)
    col_ok = (kcol[:, None, :] >= col_start[..., None]) & (kcol[:, None, :] < col_start[..., None] + WIN_C)
    dc_idx = jnp.clip(kcol[:, None, :] - qcol[..., None], -(WIN_C - 1), WIN_C - 1) + WIN_C - 1
    qg = jnp.moveaxis(q.reshape(B, rows, GRID_W, H, d), 1, 0)
    kc = k.reshape(B, rows, GRID_W, H, d)[:, :, kcol]
    vc = v.reshape(B, rows, GRID_W, H, d)[:, :, kcol]
    scale = d ** -0.5

    def row_block(args):
        r, q_row = args
        rs = jnp.clip(r - WIN_R // 2, 0, rows - wr)
        k_blk = lax.dynamic_slice_in_dim(kc, rs, wr, axis=1)
        v_blk = lax.dynamic_slice_in_dim(vc, rs, wr, axis=1)
        qb = q_row.reshape(B, n_cb, Q_BLOCK_C, H, d)
        s = jnp.einsum('bjqhd,brjkhd->bhjqrk', qb, k_blk).astype(jnp.float32) * scale
        dr_idx = rs + jnp.arange(wr) - r + WIN_R - 1
        bias = rpb[:, dr_idx[None, None, :, None], dc_idx[:, :, None, :]]
        s = s + bias.astype(jnp.float32)
        s = jnp.where(col_ok[:, :, None, :], s, -jnp.inf)
        p = jax.nn.softmax(s, axis=(-2, -1))
        o = jnp.einsum('bhjqrk,brjkhd->bjqhd', p.astype(v.dtype), v_blk)
        return o.reshape(B, GRID_W, H, d)

    out = lax.map(row_block, (jnp.arange(rows), qg))
    return jnp.moveaxis(out, 0, 1).reshape(B, T, H, d)


def _gated_delta_chunked(q, k, v, g, beta):
    B, T, H, dk = q.shape
    dv = v.shape[-1]
    n = T // CHUNK

    def blocks(t):
        t = t.reshape((B, n, CHUNK, H) + t.shape[3:])
        return jnp.moveaxis(t, 3, 1)

    q, k, v, g, beta = blocks(q), blocks(k), blocks(v), blocks(g), blocks(beta)
    gc = jnp.cumsum(g, axis=-1)
    incl = jnp.tril(jnp.ones((CHUNK, CHUNK), dtype=bool))
    strict = jnp.tril(jnp.ones((CHUNK, CHUNK), dtype=bool), -1)
    diff = gc[..., :, None] - gc[..., None, :]
    decay = jnp.where(incl, jnp.exp(jnp.where(incl, diff, 0.0)), 0.0)
    kb = k * beta[..., None]
    a_mat = jnp.where(strict, jnp.einsum('bhnid,bhnjd->bhnij', kb, k) * decay, 0.0) + jnp.eye(CHUNK, dtype=q.dtype)
    rhs = jnp.concatenate([v * beta[..., None], kb * jnp.exp(gc)[..., None]], axis=-1)
    sol = lax.linalg.triangular_solve(a_mat, rhs, left_side=True, lower=True, unit_diagonal=True)
    u, w = sol[..., :dv], sol[..., dv:]
    intra = jnp.einsum('bhnid,bhnjd->bhnij', q, k) * decay
    q_dec = q * jnp.exp(gc)[..., None]
    k_dec = k * jnp.exp(gc[..., -1:] - gc)[..., None]
    g_last = jnp.exp(gc[..., -1])

    def step(S, xs):
        u_c, w_c, q_c, k_c, a_c, gl = xs
        v_new = u_c - jnp.einsum('bhck,bhkv->bhcv', w_c, S)
        o = jnp.einsum('bhck,bhkv->bhcv', q_c, S) + jnp.einsum('bhij,bhjv->bhiv', a_c, v_new)
        S = S * gl[..., None, None] + jnp.einsum('bhck,bhcv->bhkv', k_c, v_new)
        return S, o

    xs = (jnp.moveaxis(u, 2, 0), jnp.moveaxis(w, 2, 0), jnp.moveaxis(q_dec, 2, 0),
          jnp.moveaxis(k_dec, 2, 0), jnp.moveaxis(intra, 2, 0), jnp.moveaxis(g_last, 2, 0))
    S0 = jnp.zeros((B, H, dk, dv), jnp.float32)
    _, o = lax.scan(step, S0, xs)
    return jnp.transpose(o, (1, 0, 3, 2, 4)).reshape(B, T, H, dv)


def _deltanet_branch(qkv, z, beta_raw, alpha_raw, conv_w, a_log, dt_bias, norm_g):
    B, T, C = qkv.shape
    qkv = lax.conv_general_dilated(qkv, conv_w.reshape(CONV_K, 1, C), window_strides=(1,),
                                   padding=[(CONV_K // 2, CONV_K // 2)],
                                   dimension_numbers=('NWC', 'WIO', 'NWC'), feature_group_count=C)
    qkv = jax.nn.silu(qkv).astype(jnp.float32).reshape(B, T, 3, DN_HEADS, DN_HEAD_DIM)
    q = _l2_norm(qkv[:, :, 0]) * (DN_HEAD_DIM ** -0.5)
    k = _l2_norm(qkv[:, :, 1])
    v = qkv[:, :, 2]
    beta = jax.nn.sigmoid(beta_raw.astype(jnp.float32)).reshape(B, T, N_DIR, DN_HEADS)
    g = -jnp.exp(a_log.astype(jnp.float32)) * jax.nn.softplus(
        alpha_raw.astype(jnp.float32).reshape(B, T, N_DIR, DN_HEADS) + dt_bias.astype(jnp.float32))
    o_fwd = _gated_delta_chunked(q, k, v, g[:, :, 0], beta[:, :, 0])
    o_bwd = _gated_delta_chunked(q[:, ::-1], k[:, ::-1], v[:, ::-1], g[:, ::-1, 1], beta[:, ::-1, 1])[:, ::-1]
    o = _rms_norm(o_fwd + o_bwd, norm_g).astype(z.dtype).reshape(B, T, DN_WIDTH)
    return o * jax.nn.silu(z)


def _encoder_layer(x, norm_g, w_in, q_norm_g, k_norm_g, rpb, conv_w, a_log, dt_bias, dn_norm_g,
                   w_branch_a, w_branch_b, w_out):
    B, T, _ = x.shape
    h = _rms_norm(x, norm_g)
    proj = jnp.einsum('btd,de->bte', h, w_in)
    split_points = [int(s) for s in np.cumsum(PROJ_SIZES)[:-1]]
    qkv_a, z_a, qkv_b, z_b, beta_raw, alpha_raw, gate_a, gate_b = jnp.split(proj, split_points, axis=-1)
    qkv_a = qkv_a.reshape(B, T, 3, NA_HEADS, NA_HEAD_DIM)
    q_a = _rms_norm(qkv_a[:, :, 0], q_norm_g)
    k_a = _rms_norm(qkv_a[:, :, 1], k_norm_g)
    o_a = _neighbourhood_attention(q_a, k_a, qkv_a[:, :, 2], rpb).reshape(B, T, NA_WIDTH) * jax.nn.silu(z_a)
    y_a = jnp.einsum('bte,ed->btd', o_a, w_branch_a)
    o_b = _deltanet_branch(qkv_b, z_b, beta_raw, alpha_raw, conv_w, a_log, dt_bias, dn_norm_g)
    y_b = jnp.einsum('bte,ed->btd', o_b, w_branch_b)
    merged = jax.nn.sigmoid(gate_a) * y_a + jax.nn.sigmoid(gate_b) * y_b
    return x + jnp.einsum('btd,de->bte', merged, w_out)


def _trunk(x, norm_g, w_in, attn_q_norm_g, attn_k_norm_g, attn_rpb, dn_conv_w, dn_a_log, dn_dt_bias,
           dn_norm_g, w_branch_a, w_branch_b, w_out):
    for l in range(DEPTH):
        x = _encoder_layer(x, norm_g[l], w_in[l], attn_q_norm_g[l], attn_k_norm_g[l], attn_rpb[l],
                           dn_conv_w[l], dn_a_log[l], dn_dt_bias[l], dn_norm_g[l],
                           w_branch_a[l], w_branch_b[l], w_out[l])
    return x


def setup_inputs(seed: int = 0) -> dict:
    key = jax.random.key(seed)
    ks = jax.random.split(key, 14)
    nrm = jax.random.normal
    x_prompt = nrm(ks[0], (BATCH, SEQ, D_MODEL), jnp.float32)
    x_sample = nrm(ks[1], (DEC_BATCH, DEC_SEQ, D_MODEL), jnp.float32)
    norm_g = 1.0 + 0.02 * nrm(ks[2], (DEPTH, D_MODEL), jnp.float32)
    w_in = nrm(ks[3], (DEPTH, D_MODEL, D_IN), jnp.float32) * D_MODEL ** -0.5
    attn_q_norm_g = 1.0 + 0.02 * nrm(ks[4], (DEPTH, NA_HEAD_DIM), jnp.float32)
    attn_k_norm_g = 1.0 + 0.02 * nrm(ks[5], (DEPTH, NA_HEAD_DIM), jnp.float32)
    attn_rpb = 0.02 * nrm(ks[6], (DEPTH, NA_HEADS, 2 * WIN_R - 1, 2 * WIN_C - 1), jnp.float32)
    dn_conv_w = nrm(ks[7], (DEPTH, CONV_K, 3 * DN_WIDTH), jnp.float32) * CONV_K ** -0.5
    dn_a_log = jnp.log(jax.random.uniform(ks[8], (DEPTH, N_DIR, DN_HEADS), jnp.float32, minval=1.0, maxval=16.0))
    dt = jnp.exp(jax.random.uniform(ks[9], (DEPTH, N_DIR, DN_HEADS), jnp.float32,
                                    minval=math.log(1e-3), maxval=math.log(1e-1)))
    dn_dt_bias = dt + jnp.log(-jnp.expm1(-dt))
    dn_norm_g = 1.0 + 0.02 * nrm(ks[10], (DEPTH, DN_HEAD_DIM), jnp.float32)
    w_branch_a = nrm(ks[11], (DEPTH, NA_WIDTH, D_MODEL), jnp.float32) * NA_WIDTH ** -0.5
    w_branch_b = nrm(ks[12], (DEPTH, DN_WIDTH, D_MODEL), jnp.float32) * DN_WIDTH ** -0.5
    w_out = nrm(ks[13], (DEPTH, D_MODEL, D_MODEL), jnp.float32) * D_MODEL ** -0.5
    return {"x_prompt": x_prompt, "x_sample": x_sample, "norm_g": norm_g, "w_in": w_in,
            "attn_q_norm_g": attn_q_norm_g, "attn_k_norm_g": attn_k_norm_g, "attn_rpb": attn_rpb,
            "dn_conv_w": dn_conv_w, "dn_a_log": dn_a_log, "dn_dt_bias": dn_dt_bias, "dn_norm_g": dn_norm_g,
            "w_branch_a": w_branch_a, "w_branch_b": w_branch_b, "w_out": w_out}


def reference(x_prompt, x_sample, norm_g, w_in, attn_q_norm_g, attn_k_norm_g, attn_rpb, dn_conv_w,
              dn_a_log, dn_dt_bias, dn_norm_g, w_branch_a, w_branch_b, w_out):
    y_prompt = _trunk(x_prompt, norm_g, w_in, attn_q_norm_g, attn_k_norm_g, attn_rpb, dn_conv_w,
                      dn_a_log, dn_dt_bias, dn_norm_g, w_branch_a, w_branch_b, w_out)
    y_sample = _trunk(x_sample, norm_g, w_in, attn_q_norm_g, attn_k_norm_g, attn_rpb, dn_conv_w,
                      dn_a_log, dn_dt_bias, dn_norm_g, w_branch_a, w_branch_b, w_out)
    return (y_prompt, y_sample)
```

```python
import functools

import jax
import jax.numpy as jnp
from jax import lax
from jax.experimental import pallas as pl
from jax.experimental.pallas import tpu as pltpu

D_MODEL = 1024
GRID_W = 64
NA_HEADS = 8
NA_HEAD_DIM = 64
NA_WIDTH = NA_HEADS * NA_HEAD_DIM
WIN_R = 8
WIN_C = 16
DN_HEADS = 4
DN_HEAD_DIM = 128
DN_WIDTH = DN_HEADS * DN_HEAD_DIM
CONV_K = 5
CHUNK = 64
N_DIR = 2
EPS = 1e-6

LANES = 128
HALO = 16
MASK_VALUE = -1e30

C_QKV_A = 0
C_Z_A = 3 * NA_WIDTH
C_QKV_B = C_Z_A + NA_WIDTH
C_Z_B = C_QKV_B + 3 * DN_WIDTH
C_GATE = C_Z_B + DN_WIDTH
C_BG = C_GATE + 2 * D_MODEL
C_END = C_BG + LANES

TM_PROJ = 256
ATTN_ROWS = 8
DN_BLOCK = 128
VMEM_LIMIT = 48 * 1024 * 1024

F32 = jnp.float32
BF16 = jnp.bfloat16


def _const_spec(shape):
    return pl.BlockSpec(shape, lambda *_: (0,) * len(shape), pipeline_mode=pl.Buffered(1))


def _sigmoid(x):
    return 1.0 / (1.0 + jnp.exp(-x))


def _silu(x):
    return x * _sigmoid(x)


def _dot(a, b):
    return jnp.dot(a, b, preferred_element_type=F32)


def _dot_nt(a, b):
    return lax.dot_general(a, b, (((1,), (1,)), ((), ())), preferred_element_type=F32)


def _dot_tn(a, b):
    return lax.dot_general(a, b, (((0,), (0,)), ((), ())), preferred_element_type=F32)


def _in_proj_kernel(x_ref, g_ref, w_ref, gs_ref, qg_ref, kg_ref, alog_ref, dt_ref,
                    q_ref, k_ref, v_ref, sza_ref, qkvb_ref, szb_ref, gate_ref, bg_ref):
    x = x_ref[...]
    ms = jnp.mean(x * x, axis=-1, keepdims=True)
    h = (x * lax.rsqrt(ms + EPS) * g_ref[...]).astype(BF16)
    gs = gs_ref[...]

    def proj(lo, hi):
        return _dot(h, w_ref[:, lo:hi])

    def head_norm(y, gain):
        y2 = y * y
        hi = y2.astype(BF16)
        lo = (y2 - hi.astype(F32)).astype(BF16)
        ss = _dot(hi, gs) + _dot(lo, gs)
        return y * lax.rsqrt(ss * (1.0 / NA_HEAD_DIM) + EPS) * gain

    q_ref[...] = head_norm(proj(C_QKV_A, C_QKV_A + NA_WIDTH), qg_ref[...]).astype(BF16)
    k_ref[...] = head_norm(proj(C_QKV_A + NA_WIDTH, C_QKV_A + 2 * NA_WIDTH), kg_ref[...]).astype(BF16)
    v_ref[...] = proj(C_QKV_A + 2 * NA_WIDTH, C_Z_A).astype(BF16)
    sza_ref[...] = _silu(proj(C_Z_A, C_QKV_B)).astype(BF16)
    for part in range(3):
        lo = C_QKV_B + part * DN_WIDTH
        qkvb_ref[:, part * DN_WIDTH:(part + 1) * DN_WIDTH] = proj(lo, lo + DN_WIDTH).astype(BF16)
    szb_ref[...] = _silu(proj(C_Z_B, C_GATE)).astype(BF16)
    for part in range(4):
        lo = C_GATE + part * DN_WIDTH
        gate_ref[:, part * DN_WIDTH:(part + 1) * DN_WIDTH] = _sigmoid(proj(lo, lo + DN_WIDTH)).astype(BF16)
    ba = proj(C_BG, C_END)
    lane = lax.broadcasted_iota(jnp.int32, ba.shape, 1)
    z = ba + dt_ref[...]
    sp = jnp.maximum(z, 0.0) + jnp.log1p(jnp.exp(-jnp.abs(z)))
    g = -jnp.exp(alog_ref[...]) * sp
    is_g = (lane >= N_DIR * DN_HEADS) & (lane < 2 * N_DIR * DN_HEADS)
    bg_ref[...] = jnp.where(lane < N_DIR * DN_HEADS, _sigmoid(ba), jnp.where(is_g, g, 0.0))


def _in_proj(x2, norm_g, w_re, gsum, qg, kg, alog_vec, dt_vec):
    n = x2.shape[0]
    tm = TM_PROJ
    tok = lambda width: pl.BlockSpec((tm, width), lambda i: (i, 0))
    out_shapes = (
        jax.ShapeDtypeStruct((n, NA_WIDTH), BF16),
        jax.ShapeDtypeStruct((n, NA_WIDTH), BF16),
        jax.ShapeDtypeStruct((n, NA_WIDTH), BF16),
        jax.ShapeDtypeStruct((n, NA_WIDTH), BF16),
        jax.ShapeDtypeStruct((n, 3 * DN_WIDTH), BF16),
        jax.ShapeDtypeStruct((n, DN_WIDTH), BF16),
        jax.ShapeDtypeStruct((n, 2 * D_MODEL), BF16),
        jax.ShapeDtypeStruct((n, LANES), F32),
    )
    return pl.pallas_call(
        _in_proj_kernel,
        grid=(n // tm,),
        in_specs=[tok(D_MODEL), _const_spec((1, D_MODEL)), _const_spec((D_MODEL, C_END)),
                  _const_spec((NA_WIDTH, NA_WIDTH)), _const_spec((1, NA_WIDTH)), _const_spec((1, NA_WIDTH)),
                  _const_spec((1, LANES)), _const_spec((1, LANES))],
        out_specs=[tok(NA_WIDTH), tok(NA_WIDTH), tok(NA_WIDTH), tok(NA_WIDTH), tok(3 * DN_WIDTH),
                   tok(DN_WIDTH), tok(2 * D_MODEL), tok(LANES)],
        out_shape=out_shapes,
        compiler_params=pltpu.CompilerParams(dimension_semantics=("arbitrary",),
                                             vmem_limit_bytes=VMEM_LIMIT),
        name="in_proj",
    )(x2, norm_g, w_re, gsum, qg, kg, alog_vec, dt_vec)


def _attn_kernel(q_ref, k_ref, v_ref, sza_ref, bias_ref, o_ref, *, rows):
    blk = pl.program_id(1)
    lane = lax.broadcasted_iota(jnp.int32, (GRID_W, LANES), 1)
    lo_half = lane < NA_HEAD_DIM
    n_keys = WIN_R * GRID_W

    def row_body(rl, carry):
        r = blk * ATTN_ROWS + rl
        rs = jnp.clip(r - WIN_R // 2, 0, rows - WIN_R)
        var = r - rs
        kstart = pl.multiple_of(rs * GRID_W, GRID_W)
        qstart = pl.multiple_of(rl * GRID_W, GRID_W)
        for p in range(NA_HEADS // 2):
            cols = slice(p * LANES, (p + 1) * LANES)
            q2 = q_ref[0, pl.ds(qstart, GRID_W), cols]
            k2 = k_ref[0, pl.ds(kstart, n_keys), cols]
            v2 = v_ref[0, pl.ds(kstart, n_keys), cols]
            outs = []
            for e in range(2):
                qm = jnp.where(lo_half if e == 0 else jnp.logical_not(lo_half), q2, jnp.zeros_like(q2))
                s = _dot_nt(qm, k2) + bias_ref[2 * p + e, var]
                m = jnp.max(s, axis=-1, keepdims=True)
                pe = jnp.exp(s - m)
                l = jnp.sum(pe, axis=-1, keepdims=True)
                outs.append(_dot(pe.astype(BF16), v2) / l)
            o2 = jnp.where(lo_half, outs[0], outs[1])
            sz = sza_ref[0, pl.ds(qstart, GRID_W), cols].astype(F32)
            o_ref[0, pl.ds(qstart, GRID_W), cols] = (o2 * sz).astype(BF16)
        return carry

    lax.fori_loop(0, ATTN_ROWS, row_body, 0)


def _attention(q, k, v, sza, bias):
    b, t, _ = q.shape
    rows = t // GRID_W
    assert rows >= WIN_R and rows % ATTN_ROWS == 0
    tq = ATTN_ROWS * GRID_W
    blk = pl.BlockSpec((1, tq, NA_WIDTH), lambda bi, i: (bi, i, 0))
    full = pl.BlockSpec((1, t, NA_WIDTH), lambda bi, i: (bi, 0, 0), pipeline_mode=pl.Buffered(1))
    return pl.pallas_call(
        functools.partial(_attn_kernel, rows=rows),
        grid=(b, rows // ATTN_ROWS),
        in_specs=[blk, full, full, blk, _const_spec(bias.shape)],
        out_specs=blk,
        out_shape=jax.ShapeDtypeStruct((b, t, NA_WIDTH), BF16),
        compiler_params=pltpu.CompilerParams(dimension_semantics=("arbitrary", "arbitrary"),
                                             vmem_limit_bytes=VMEM_LIMIT),
        name="attn",
    )(q, k, v, sza, bias)


def _attn_bias_table(rpb):
    var = jnp.arange(WIN_R)
    kr = jnp.arange(WIN_R)
    dr_idx = kr[None, :] - var[:, None] + WIN_R - 1
    qc = jnp.arange(GRID_W)
    kc = jnp.arange(GRID_W)
    col_start = jnp.clip(qc - WIN_C // 2, 0, GRID_W - WIN_C)
    ok = (kc[None, :] >= col_start[:, None]) & (kc[None, :] < col_start[:, None] + WIN_C)
    dc_idx = jnp.clip(kc[None, :] - qc[:, None], -(WIN_C - 1), WIN_C - 1) + WIN_C - 1
    tab = rpb[:, dr_idx[:, :, None, None], dc_idx[None, None, :, :]]
    tab = jnp.where(ok[None, None, None], tab.astype(F32), MASK_VALUE)
    tab = jnp.transpose(tab, (0, 1, 3, 2, 4))
    return tab.reshape(NA_HEADS, WIN_R, GRID_W, WIN_R * GRID_W)


def _split3_dot(tri, x):
    x1 = x.astype(BF16)
    r1 = x - x1.astype(F32)
    x2 = r1.astype(BF16)
    x3 = (r1 - x2.astype(F32)).astype(BF16)
    return _dot(tri, x1) + _dot(tri, x2) + _dot(tri, x3)


def _dn_kernel(mf_ref, pf_ref, nf_ref, mb_ref, pb_ref, nb_ref, bgf_ref, bgb_ref, cw_ref, tl_ref, tu_ref,
               of_ref, ob_ref, s_ref, xe_ref, *, nblk):
    i = pl.program_id(1)
    tb = DN_BLOCK
    nc = tb // CHUNK

    @pl.when(i == 0)
    def _():
        s_ref[...] = jnp.zeros_like(s_ref)

    ii = lax.broadcasted_iota(jnp.int32, (CHUNK, CHUNK), 0)
    jj = lax.broadcasted_iota(jnp.int32, (CHUNK, CHUNK), 1)
    eye = (ii == jj).astype(F32)
    lane = lax.broadcasted_iota(jnp.int32, (tb, LANES), 1)
    is_g = (lane >= N_DIR * DN_HEADS) & (lane < 2 * N_DIR * DN_HEADS)

    for d in range(N_DIR):
        main_ref, prev_ref, next_ref = (mf_ref, pf_ref, nf_ref) if d == 0 else (mb_ref, pb_ref, nb_ref)
        bg_ref = bgf_ref if d == 0 else bgb_ref
        o_ref = of_ref if d == 0 else ob_ref
        blk = i if d == 0 else nblk - 1 - i
        incl = (ii >= jj) if d == 0 else (ii <= jj)
        strict = (ii > jj) if d == 0 else (ii < jj)

        pv = jnp.where(blk > 0, 1.0, 0.0).astype(F32)
        nv = jnp.where(blk < nblk - 1, 1.0, 0.0).astype(F32)
        xe_ref[d, 0:8, :] = prev_ref[0, HALO - 8:HALO, :].astype(F32) * pv
        xe_ref[d, 8:8 + tb, :] = main_ref[0].astype(F32)
        xe_ref[d, 8 + tb:16 + tb, :] = next_ref[0, 0:8, :].astype(F32) * nv

        def conv(col, d=d):
            acc = None
            for j in range(CONV_K):
                start = 8 - CONV_K // 2 + j
                term = xe_ref[d, start:start + tb, col:col + LANES] * cw_ref[j:j + 1, col:col + LANES]
                acc = term if acc is None else acc + term
            return _silu(acc)

        def l2n(x):
            return x * lax.rsqrt(jnp.sum(x * x, axis=-1, keepdims=True) + EPS)

        bg = bg_ref[0]
        gvals = jnp.where(is_g, bg, 0.0)
        gc = _split3_dot(tl_ref[...] if d == 0 else tu_ref[...], gvals)
        gct = gc.T
        eg = jnp.exp(gc)

        for h in range(DN_HEADS):
            bi = d * DN_HEADS + h
            ci = N_DIR * DN_HEADS + bi
            qh = l2n(conv(h * DN_HEAD_DIM)) * (DN_HEAD_DIM ** -0.5)
            kh = l2n(conv(DN_WIDTH + h * DN_HEAD_DIM))
            vh = conv(2 * DN_WIDTH + h * DN_HEAD_DIM)
            beta = bg[:, bi:bi + 1]
            gcol_all = gc[:, ci:ci + 1]
            egcol = eg[:, ci:ci + 1]
            kb = kh * beta
            rhs_all = jnp.concatenate([vh * beta, kb * egcol], axis=1).astype(BF16)
            qdec_all = (qh * egcol).astype(BF16)
            k_bf = kh.astype(BF16)
            q_bf = qh.astype(BF16)
            chunk_order = range(nc) if d == 0 else range(nc - 1, -1, -1)
            for c in chunk_order:
                rc = slice(c * CHUNK, (c + 1) * CHUNK)
                kc = k_bf[rc]
                kk = _dot_nt(kc, kc)
                qk = _dot_nt(q_bf[rc], kc)
                gcol = gcol_all[rc]
                grow = gct[ci:ci + 1, rc]
                diff = gcol - grow
                decay = jnp.where(incl, jnp.exp(jnp.where(incl, diff, 0.0)), 0.0)
                a = jnp.where(strict, kk * beta[rc] * decay, 0.0)
                m = -a
                p = eye + m
                x = m.astype(BF16)
                for _ in range(5):
                    x = _dot(x, x).astype(BF16)
                    p = p + _dot(p.astype(BF16), x)
                sol = _dot(p.astype(BF16), rhs_all[rc])
                u = sol[:, :DN_HEAD_DIM]
                w = sol[:, DN_HEAD_DIM:]
                intra = (qk * decay).astype(BF16)
                glast = gcol[CHUNK - 1:CHUNK] if d == 0 else gcol[0:1]
                kdec = (kh[rc] * jnp.exp(glast - gcol)).astype(BF16)
                s = s_ref[bi]
                wq = jnp.concatenate([w.astype(BF16), qdec_all[rc]], axis=0)
                res = _dot(wq, s.astype(BF16))
                v_new = (u - res[:CHUNK]).astype(BF16)
                o = res[CHUNK:] + _dot(intra, v_new)
                s_ref[bi] = s * jnp.exp(glast) + _dot_tn(kdec, v_new)
                o_ref[0, rc, h * DN_HEAD_DIM:(h + 1) * DN_HEAD_DIM] = o.astype(o_ref.dtype)


def _deltanet(qkvb, bg, conv_w8, tri_l, tri_u):
    b, t, _ = qkvb.shape
    tb = DN_BLOCK
    nblk = t // tb
    hpb = tb // HALO
    nh = t // HALO
    c3 = 3 * DN_WIDTH
    fwd = lambda bi, i: (bi, i, 0)
    bwd = lambda bi, i: (bi, nblk - 1 - i, 0)
    prev_f = lambda bi, i: (bi, jnp.maximum(i * hpb - 1, 0), 0)
    next_f = lambda bi, i: (bi, jnp.minimum((i + 1) * hpb, nh - 1), 0)
    prev_b = lambda bi, i: (bi, jnp.maximum((nblk - 1 - i) * hpb - 1, 0), 0)
    next_b = lambda bi, i: (bi, jnp.minimum((nblk - i) * hpb, nh - 1), 0)
    main = lambda f: pl.BlockSpec((1, tb, c3), f)
    halo = lambda f: pl.BlockSpec((1, HALO, c3), f)
    return pl.pallas_call(
        functools.partial(_dn_kernel, nblk=nblk),
        grid=(b, nblk),
        in_specs=[main(fwd), halo(prev_f), halo(next_f), main(bwd), halo(prev_b), halo(next_b),
                  pl.BlockSpec((1, tb, LANES), fwd), pl.BlockSpec((1, tb, LANES), bwd),
                  _const_spec((8, c3)), _const_spec((tb, tb)), _const_spec((tb, tb))],
        out_specs=[pl.BlockSpec((1, tb, DN_WIDTH), fwd), pl.BlockSpec((1, tb, DN_WIDTH), bwd)],
        out_shape=(jax.ShapeDtypeStruct((b, t, DN_WIDTH), BF16),
                   jax.ShapeDtypeStruct((b, t, DN_WIDTH), BF16)),
        scratch_shapes=[pltpu.VMEM((N_DIR * DN_HEADS, DN_HEAD_DIM, DN_HEAD_DIM), F32),
                        pltpu.VMEM((N_DIR, tb + 16, c3), F32)],
        compiler_params=pltpu.CompilerParams(dimension_semantics=("arbitrary", "arbitrary"),
                                             vmem_limit_bytes=VMEM_LIMIT),
        name="deltanet",
    )(qkvb, qkvb, qkvb, qkvb, qkvb, qkvb, bg, bg, conv_w8, tri_l, tri_u)


def _out_kernel(x_ref, oa_ref, of_ref, ob_ref, szb_ref, gate_ref, ng_ref, wa_ref, wb_ref, wo_ref, y_ref):
    o = of_ref[...].astype(F32) + ob_ref[...].astype(F32)
    parts = []
    for h in range(DN_HEADS):
        oh = o[:, h * DN_HEAD_DIM:(h + 1) * DN_HEAD_DIM]
        ms = jnp.mean(oh * oh, axis=-1, keepdims=True)
        parts.append(oh * lax.rsqrt(ms + EPS) * ng_ref[...])
    on = jnp.concatenate(parts, axis=1)
    o_b = (on * szb_ref[...].astype(F32)).astype(BF16)
    ya = _dot(oa_ref[...], wa_ref[...])
    yb = _dot(o_b, wb_ref[...])
    merged = (gate_ref[:, :D_MODEL].astype(F32) * ya + gate_ref[:, D_MODEL:].astype(F32) * yb).astype(BF16)
    y_ref[...] = x_ref[...] + _dot(merged, wo_ref[...])


def _out_stage(x2, oa, of, ob, szb, gates, ng, wa, wb, wo):
    n = x2.shape[0]
    tm = TM_PROJ
    tok = lambda width: pl.BlockSpec((tm, width), lambda i: (i, 0))
    return pl.pallas_call(
        _out_kernel,
        grid=(n // tm,),
        in_specs=[tok(D_MODEL), tok(NA_WIDTH), tok(DN_WIDTH), tok(DN_WIDTH), tok(DN_WIDTH), tok(2 * D_MODEL),
                  _const_spec((1, DN_HEAD_DIM)), _const_spec((NA_WIDTH, D_MODEL)),
                  _const_spec((DN_WIDTH, D_MODEL)), _const_spec((D_MODEL, D_MODEL))],
        out_specs=tok(D_MODEL),
        out_shape=jax.ShapeDtypeStruct((n, D_MODEL), F32),
        compiler_params=pltpu.CompilerParams(dimension_semantics=("arbitrary",),
                                             vmem_limit_bytes=VMEM_LIMIT),
        name="out_stage",
    )(x2, oa, of, ob, szb, gates, ng, wa, wb, wo)


def _layer_params(l, norm_g, w_in, attn_q_norm_g, attn_k_norm_g, attn_rpb, dn_conv_w, dn_a_log, dn_dt_bias,
                  dn_norm_g, w_branch_a, w_branch_b, w_out):
    n_ba = 2 * N_DIR * DN_HEADS
    w = w_in[l]
    c_ba = 3 * NA_WIDTH + NA_WIDTH + 3 * DN_WIDTH + DN_WIDTH
    w_re = jnp.concatenate(
        [w[:, :c_ba], w[:, c_ba + n_ba:], w[:, c_ba:c_ba + n_ba],
         jnp.zeros((D_MODEL, LANES - n_ba), w.dtype)], axis=1).astype(BF16)
    pad = lambda vec, lo: jnp.zeros((1, LANES), F32).at[0, lo:lo + vec.size].set(vec.reshape(-1).astype(F32))
    head_id = jnp.arange(NA_WIDTH) // NA_HEAD_DIM
    tok_id = jnp.arange(DN_BLOCK)
    same_chunk = (tok_id[:, None] // CHUNK) == (tok_id[None, :] // CHUNK)
    return dict(
        norm_g=norm_g[l].reshape(1, D_MODEL),
        w_re=w_re,
        gsum=(head_id[:, None] == head_id[None, :]).astype(BF16),
        qg=(jnp.tile(attn_q_norm_g[l], NA_HEADS) * NA_HEAD_DIM ** -0.5).reshape(1, NA_WIDTH),
        kg=jnp.tile(attn_k_norm_g[l], NA_HEADS).reshape(1, NA_WIDTH),
        alog=pad(dn_a_log[l], N_DIR * DN_HEADS),
        dt=pad(dn_dt_bias[l], N_DIR * DN_HEADS),
        bias=_attn_bias_table(attn_rpb[l]),
        conv_w8=jnp.concatenate([dn_conv_w[l].astype(F32), jnp.zeros((8 - CONV_K, 3 * DN_WIDTH), F32)], axis=0),
        tri_l=(same_chunk & (tok_id[:, None] >= tok_id[None, :])).astype(BF16),
        tri_u=(same_chunk & (tok_id[:, None] <= tok_id[None, :])).astype(BF16),
        dn_g=dn_norm_g[l].reshape(1, DN_HEAD_DIM),
        wa=w_branch_a[l].astype(BF16),
        wb=w_branch_b[l].astype(BF16),
        wo=w_out[l].astype(BF16),
    )


def _layer(x, p):
    b, t, _ = x.shape
    x2 = x.reshape(b * t, D_MODEL)
    q, k, v, sza, qkvb, szb, gates, bg = _in_proj(x2, p["norm_g"], p["w_re"], p["gsum"], p["qg"], p["kg"],
                                                  p["alog"], p["dt"])
    r3 = lambda a: a.reshape(b, t, a.shape[-1])
    oa = _attention(r3(q), r3(k), r3(v), r3(sza), p["bias"])
    of, ob = _deltanet(r3(qkvb), r3(bg), p["conv_w8"], p["tri_l"], p["tri_u"])
    y = _out_stage(x2, oa.reshape(b * t, NA_WIDTH), of.reshape(b * t, DN_WIDTH), ob.reshape(b * t, DN_WIDTH),
                   szb, gates, p["dn_g"], p["wa"], p["wb"], p["wo"])
    return y.reshape(b, t, D_MODEL)


def kernel(x_prompt, x_sample, norm_g, w_in, attn_q_norm_g, attn_k_norm_g, attn_rpb, dn_conv_w, dn_a_log,
           dn_dt_bias, dn_norm_g, w_branch_a, w_branch_b, w_out):
    depth = w_in.shape[0]
    params = [_layer_params(l, norm_g, w_in, attn_q_norm_g, attn_k_norm_g, attn_rpb, dn_conv_w, dn_a_log,
                            dn_dt_bias, dn_norm_g, w_branch_a, w_branch_b, w_out) for l in range(depth)]
    outs = []
    for x in (x_prompt, x_sample):
        for p in params:
            x = _layer(x, p)
        outs.append(x)
    return tuple(outs)
```

```python
import functools

import jax
import jax.numpy as jnp
from jax import lax
from jax.experimental import pallas as pl
from jax.experimental.pallas import tpu as pltpu

D_MODEL = 1024
GRID_W = 64
NA_HEADS = 8
NA_HEAD_DIM = 64
NA_WIDTH = NA_HEADS * NA_HEAD_DIM
WIN_R = 8
WIN_C = 16
DN_HEADS = 4
DN_HEAD_DIM = 128
DN_WIDTH = DN_HEADS * DN_HEAD_DIM
CONV_K = 5
CHUNK = 64
N_DIR = 2
EPS = 1e-6

LANES = 128
HALO = 16
MASK_VALUE = -1e30

C_QKV_A = 0
C_Z_A = 3 * NA_WIDTH
C_QKV_B = C_Z_A + NA_WIDTH
C_Z_B = C_QKV_B + 3 * DN_WIDTH
C_GATE = C_Z_B + DN_WIDTH
C_BG = C_GATE + 2 * D_MODEL
C_END = C_BG + LANES

TM_PROJ = 256
ATTN_ROWS = 8
DN_BLOCK = 256
VMEM_LIMIT = 48 * 1024 * 1024

F32 = jnp.float32
BF16 = jnp.bfloat16


def _const_spec(shape):
    return pl.BlockSpec(shape, lambda *_: (0,) * len(shape), pipeline_mode=pl.Buffered(1))


def _sigmoid(x):
    return 1.0 / (1.0 + jnp.exp(-x))


def _silu(x):
    return x * _sigmoid(x)


def _dot(a, b):
    return jnp.dot(a, b, preferred_element_type=F32)


def _dot_nt(a, b):
    return lax.dot_general(a, b, (((1,), (1,)), ((), ())), preferred_element_type=F32)


def _dot_tn(a, b):
    return lax.dot_general(a, b, (((0,), (0,)), ((), ())), preferred_element_type=F32)


def _in_proj_kernel(x_ref, g_ref, w_ref, gs_ref, qg_ref, kg_ref, alog_ref, dt_ref,
                    q_ref, k_ref, v_ref, sza_ref, qkvb_ref, szb_ref, gate_ref, bg_ref):
    x = x_ref[...]
    ms = jnp.mean(x * x, axis=-1, keepdims=True)
    h = (x * lax.rsqrt(ms + EPS) * g_ref[...]).astype(BF16)
    gs = gs_ref[...]

    def proj(lo, hi):
        return _dot(h, w_ref[:, lo:hi])

    def head_norm(y, gain):
        y2 = y * y
        hi = y2.astype(BF16)
        lo = (y2 - hi.astype(F32)).astype(BF16)
        ss = _dot(hi, gs) + _dot(lo, gs)
        return y * lax.rsqrt(ss * (1.0 / NA_HEAD_DIM) + EPS) * gain

    q_ref[...] = head_norm(proj(C_QKV_A, C_QKV_A + NA_WIDTH), qg_ref[...]).astype(BF16)
    k_ref[...] = head_norm(proj(C_QKV_A + NA_WIDTH, C_QKV_A + 2 * NA_WIDTH), kg_ref[...]).astype(BF16)
    v_ref[...] = proj(C_QKV_A + 2 * NA_WIDTH, C_Z_A).astype(BF16)
    sza_ref[...] = _silu(proj(C_Z_A, C_QKV_B)).astype(BF16)
    for part in range(3):
        lo = C_QKV_B + part * DN_WIDTH
        qkvb_ref[:, part * DN_WIDTH:(part + 1) * DN_WIDTH] = proj(lo, lo + DN_WIDTH).astype(BF16)
    szb_ref[...] = _silu(proj(C_Z_B, C_GATE)).astype(BF16)
    for part in range(4):
        lo = C_GATE + part * DN_WIDTH
        gate_ref[:, part * DN_WIDTH:(part + 1) * DN_WIDTH] = _sigmoid(proj(lo, lo + DN_WIDTH)).astype(BF16)
    ba = proj(C_BG, C_END)
    lane = lax.broadcasted_iota(jnp.int32, ba.shape, 1)
    z = ba + dt_ref[...]
    sp = jnp.maximum(z, 0.0) + jnp.log1p(jnp.exp(-jnp.abs(z)))
    g = -jnp.exp(alog_ref[...]) * sp
    is_g = (lane >= N_DIR * DN_HEADS) & (lane < 2 * N_DIR * DN_HEADS)
    bg_ref[...] = jnp.where(lane < N_DIR * DN_HEADS, _sigmoid(ba), jnp.where(is_g, g, 0.0))


def _in_proj(x2, norm_g, w_re, gsum, qg, kg, alog_vec, dt_vec):
    n = x2.shape[0]
    tm = TM_PROJ
    tok = lambda width: pl.BlockSpec((tm, width), lambda i: (i, 0))
    out_shapes = (
        jax.ShapeDtypeStruct((n, NA_WIDTH), BF16),
        jax.ShapeDtypeStruct((n, NA_WIDTH), BF16),
        jax.ShapeDtypeStruct((n, NA_WIDTH), BF16),
        jax.ShapeDtypeStruct((n, NA_WIDTH), BF16),
        jax.ShapeDtypeStruct((n, 3 * DN_WIDTH), BF16),
        jax.ShapeDtypeStruct((n, DN_WIDTH), BF16),
        jax.ShapeDtypeStruct((n, 2 * D_MODEL), BF16),
        jax.ShapeDtypeStruct((n, LANES), F32),
    )
    return pl.pallas_call(
        _in_proj_kernel,
        grid=(n // tm,),
        in_specs=[tok(D_MODEL), _const_spec((1, D_MODEL)), _const_spec((D_MODEL, C_END)),
                  _const_spec((NA_WIDTH, NA_WIDTH)), _const_spec((1, NA_WIDTH)), _const_spec((1, NA_WIDTH)),
                  _const_spec((1, LANES)), _const_spec((1, LANES))],
        out_specs=[tok(NA_WIDTH), tok(NA_WIDTH), tok(NA_WIDTH), tok(NA_WIDTH), tok(3 * DN_WIDTH),
                   tok(DN_WIDTH), tok(2 * D_MODEL), tok(LANES)],
        out_shape=out_shapes,
        compiler_params=pltpu.CompilerParams(dimension_semantics=("arbitrary",),
                                             vmem_limit_bytes=VMEM_LIMIT),
        name="in_proj",
    )(x2, norm_g, w_re, gsum, qg, kg, alog_vec, dt_vec)


def _attn_kernel(q_ref, k_ref, v_ref, sza_ref, bias_ref, o_ref, *, rows):
    blk = pl.program_id(1)
    lane = lax.broadcasted_iota(jnp.int32, (GRID_W, LANES), 1)
    lo_half = lane < NA_HEAD_DIM
    n_keys = WIN_R * GRID_W

    def row_body(rl, carry):
        r = blk * ATTN_ROWS + rl
        rs = jnp.clip(r - WIN_R // 2, 0, rows - WIN_R)
        var = r - rs
        kstart = pl.multiple_of(rs * GRID_W, GRID_W)
        qstart = pl.multiple_of(rl * GRID_W, GRID_W)
        for p in range(NA_HEADS // 2):
            cols = slice(p * LANES, (p + 1) * LANES)
            q2 = q_ref[0, pl.ds(qstart, GRID_W), cols]
            k2 = k_ref[0, pl.ds(kstart, n_keys), cols]
            v2 = v_ref[0, pl.ds(kstart, n_keys), cols]
            outs = []
            for e in range(2):
                qm = jnp.where(lo_half if e == 0 else jnp.logical_not(lo_half), q2, jnp.zeros_like(q2))
                s = _dot_nt(qm, k2) + bias_ref[2 * p + e, var]
                m = jnp.max(s, axis=-1, keepdims=True)
                pe = jnp.exp(s - m)
                l = jnp.sum(pe, axis=-1, keepdims=True)
                outs.append(_dot(pe.astype(BF16), v2) / l)
            o2 = jnp.where(lo_half, outs[0], outs[1])
            sz = sza_ref[0, pl.ds(qstart, GRID_W), cols].astype(F32)
            o_ref[0, pl.ds(qstart, GRID_W), cols] = (o2 * sz).astype(BF16)
        return carry

    lax.fori_loop(0, ATTN_ROWS, row_body, 0)


def _attention(q, k, v, sza, bias):
    b, t, _ = q.shape
    rows = t // GRID_W
    assert rows >= WIN_R and rows % ATTN_ROWS == 0
    tq = ATTN_ROWS * GRID_W
    blk = pl.BlockSpec((1, tq, NA_WIDTH), lambda bi, i: (bi, i, 0))
    full = pl.BlockSpec((1, t, NA_WIDTH), lambda bi, i: (bi, 0, 0), pipeline_mode=pl.Buffered(1))
    return pl.pallas_call(
        functools.partial(_attn_kernel, rows=rows),
        grid=(b, rows // ATTN_ROWS),
        in_specs=[blk, full, full, blk, _const_spec(bias.shape)],
        out_specs=blk,
        out_shape=jax.ShapeDtypeStruct((b, t, NA_WIDTH), BF16),
        compiler_params=pltpu.CompilerParams(dimension_semantics=("arbitrary", "arbitrary"),
                                             vmem_limit_bytes=VMEM_LIMIT),
        name="attn",
    )(q, k, v, sza, bias)


def _attn_bias_table(rpb):
    n_dr = 2 * WIN_R - 1
    n_dc = 2 * WIN_C - 1
    qc = jnp.arange(GRID_W)
    kc = jnp.arange(GRID_W)
    col_start = jnp.clip(qc - WIN_C // 2, 0, GRID_W - WIN_C)
    ok = (kc[None, :] >= col_start[:, None]) & (kc[None, :] < col_start[:, None] + WIN_C)
    period = 2 * GRID_W
    lead = GRID_W - WIN_C
    ext = jnp.pad(rpb.astype(F32), ((0, 0), (0, 0), (lead, period - lead - n_dc)))
    flat = jnp.tile(ext, (1, 1, GRID_W))[:, :, :GRID_W * (period - 1)]
    band = flat.reshape(NA_HEADS, n_dr, GRID_W, period - 1)[:, :, :, GRID_W - 1:]
    band = jnp.where(ok[None, None], band, MASK_VALUE)
    tab = jnp.stack([band[:, WIN_R - 1 - var:2 * WIN_R - 1 - var] for var in range(WIN_R)], axis=1)
    tab = jnp.transpose(tab, (0, 1, 3, 2, 4))
    return tab.reshape(NA_HEADS, WIN_R, GRID_W, WIN_R * GRID_W)


def _split3_dot(tri, x):
    x1 = x.astype(BF16)
    r1 = x - x1.astype(F32)
    x2 = r1.astype(BF16)
    x3 = (r1 - x2.astype(F32)).astype(BF16)
    return _dot(tri, x1) + _dot(tri, x2) + _dot(tri, x3)


def _dn_kernel(mf_ref, pf_ref, nf_ref, mb_ref, pb_ref, nb_ref, bgf_ref, bgb_ref, cw_ref, tl_ref, tu_ref,
               of_ref, ob_ref, s_ref, xe_ref, *, nblk):
    i = pl.program_id(1)
    tb = DN_BLOCK
    nc = tb // CHUNK

    @pl.when(i == 0)
    def _():
        s_ref[...] = jnp.zeros_like(s_ref)

    ii = lax.broadcasted_iota(jnp.int32, (tb, tb), 0)
    jj = lax.broadcasted_iota(jnp.int32, (tb, tb), 1)
    same = (ii // CHUNK) == (jj // CHUNK)
    eye = (ii == jj).astype(F32)
    lane = lax.broadcasted_iota(jnp.int32, (tb, LANES), 1)
    is_g = (lane >= N_DIR * DN_HEADS) & (lane < 2 * N_DIR * DN_HEADS)

    chains = []
    for d in range(N_DIR):
        main_ref, prev_ref, next_ref = (mf_ref, pf_ref, nf_ref) if d == 0 else (mb_ref, pb_ref, nb_ref)
        bg_ref = bgf_ref if d == 0 else bgb_ref
        blk = i if d == 0 else nblk - 1 - i
        incl = same & ((ii >= jj) if d == 0 else (ii <= jj))
        strict = same & ((ii > jj) if d == 0 else (ii < jj))

        pv = jnp.where(blk > 0, 1.0, 0.0).astype(F32)
        nv = jnp.where(blk < nblk - 1, 1.0, 0.0).astype(F32)
        xe_ref[d, 0:8, :] = prev_ref[0, HALO - 8:HALO, :].astype(F32) * pv
        xe_ref[d, 8:8 + tb, :] = main_ref[0].astype(F32)
        xe_ref[d, 8 + tb:16 + tb, :] = next_ref[0, 0:8, :].astype(F32) * nv

        def conv(col, d=d):
            acc = None
            for j in range(CONV_K):
                start = 8 - CONV_K // 2 + j
                term = xe_ref[d, start:start + tb, col:col + LANES] * cw_ref[j:j + 1, col:col + LANES]
                acc = term if acc is None else acc + term
            return _silu(acc)

        def l2n(x):
            return x * lax.rsqrt(jnp.sum(x * x, axis=-1, keepdims=True) + EPS)

        bg = bg_ref[0]
        gvals = jnp.where(is_g, bg, 0.0)
        gc = _split3_dot(tl_ref[...] if d == 0 else tu_ref[...], gvals)
        gct = gc.T
        eg = jnp.exp(gc)

        for h in range(DN_HEADS):
            bi = d * DN_HEADS + h
            ci = N_DIR * DN_HEADS + bi
            qh = l2n(conv(h * DN_HEAD_DIM)) * (DN_HEAD_DIM ** -0.5)
            kh = l2n(conv(DN_WIDTH + h * DN_HEAD_DIM))
            vh = conv(2 * DN_WIDTH + h * DN_HEAD_DIM)
            beta = bg[:, bi:bi + 1]
            gcol_all = gc[:, ci:ci + 1]
            egcol = eg[:, ci:ci + 1]
            kb = kh * beta
            rhs = jnp.concatenate([kb * egcol, vh * beta], axis=1).astype(BF16)
            k_bf = kh.astype(BF16)
            kk = _dot_nt(k_bf, k_bf)
            qk = _dot_nt(qh.astype(BF16), k_bf)
            diff = gcol_all - gct[ci:ci + 1, :]
            decay = jnp.where(incl, jnp.exp(jnp.where(incl, diff, 0.0)), 0.0)
            glast, kdec = [], []
            for c in range(nc):
                rc = slice(c * CHUNK, (c + 1) * CHUNK)
                row = (c + 1) * CHUNK - 1 if d == 0 else c * CHUNK
                gl = gcol_all[row:row + 1]
                glast.append(gl)
                kdec.append((kh[rc] * jnp.exp(gl - gcol_all[rc])).astype(BF16))
            chains.append(dict(
                d=d, h=h, bi=bi,
                m=jnp.where(strict, -(kk * beta * decay), 0.0),
                intra=(qk * decay).astype(BF16),
                rhs=rhs, qdec=qh * egcol, glast=glast, kdec=kdec))

    xs = [ch["m"].astype(BF16) for ch in chains]
    ps = [eye + ch["m"] for ch in chains]
    for _ in range(5):
        xs = [_dot(x, x).astype(BF16) for x in xs]
        ps = [p + _dot(p.astype(BF16), x) for p, x in zip(ps, xs)]

    for ch, p in zip(chains, ps):
        wu = _dot(p.astype(BF16), ch["rhs"]).astype(BF16)
        iw = _dot(ch["intra"], wu)
        ch["qp"] = (ch["qdec"] - iw[:, :DN_HEAD_DIM]).astype(BF16)
        ch["oi"] = iw[:, DN_HEAD_DIM:]
        ch["gh"] = [_dot_tn(ch["kdec"][c], wu[c * CHUNK:(c + 1) * CHUNK]) for c in range(nc)]
        ch["s"] = s_ref[ch["bi"]]

    for step in range(nc):
        for ch in chains:
            c = step if ch["d"] == 0 else nc - 1 - step
            rc = slice(c * CHUNK, (c + 1) * CHUNK)
            o_ref = of_ref if ch["d"] == 0 else ob_ref
            s = ch["s"]
            sb = s.astype(BF16)
            o = _dot(ch["qp"][rc], sb) + ch["oi"][rc]
            hs = slice(ch["h"] * DN_HEAD_DIM, (ch["h"] + 1) * DN_HEAD_DIM)
            o_ref[0, rc, hs] = o.astype(o_ref.dtype)
            gh = ch["gh"][c]
            ch["s"] = (s * jnp.exp(ch["glast"][c]) + gh[:, DN_HEAD_DIM:]
                       - _dot(gh[:, :DN_HEAD_DIM].astype(BF16), sb))

    for ch in chains:
        s_ref[ch["bi"]] = ch["s"]


def _deltanet(qkvb, bg, conv_w8, tri_l, tri_u):
    b, t, _ = qkvb.shape
    tb = DN_BLOCK
    nblk = t // tb
    hpb = tb // HALO
    nh = t // HALO
    c3 = 3 * DN_WIDTH
    fwd = lambda bi, i: (bi, i, 0)
    bwd = lambda bi, i: (bi, nblk - 1 - i, 0)
    prev_f = lambda bi, i: (bi, jnp.maximum(i * hpb - 1, 0), 0)
    next_f = lambda bi, i: (bi, jnp.minimum((i + 1) * hpb, nh - 1), 0)
    prev_b = lambda bi, i: (bi, jnp.maximum((nblk - 1 - i) * hpb - 1, 0), 0)
    next_b = lambda bi, i: (bi, jnp.minimum((nblk - i) * hpb, nh - 1), 0)
    main = lambda f: pl.BlockSpec((1, tb, c3), f)
    halo = lambda f: pl.BlockSpec((1, HALO, c3), f)
    return pl.pallas_call(
        functools.partial(_dn_kernel, nblk=nblk),
        grid=(b, nblk),
        in_specs=[main(fwd), halo(prev_f), halo(next_f), main(bwd), halo(prev_b), halo(next_b),
                  pl.BlockSpec((1, tb, LANES), fwd), pl.BlockSpec((1, tb, LANES), bwd),
                  _const_spec((8, c3)), _const_spec((tb, tb)), _const_spec((tb, tb))],
        out_specs=[pl.BlockSpec((1, tb, DN_WIDTH), fwd), pl.BlockSpec((1, tb, DN_WIDTH), bwd)],
        out_shape=(jax.ShapeDtypeStruct((b, t, DN_WIDTH), BF16),
                   jax.ShapeDtypeStruct((b, t, DN_WIDTH), BF16)),
        scratch_shapes=[pltpu.VMEM((N_DIR * DN_HEADS, DN_HEAD_DIM, DN_HEAD_DIM), F32),
                        pltpu.VMEM((N_DIR, tb + 16, c3), F32)],
        compiler_params=pltpu.CompilerParams(dimension_semantics=("arbitrary", "arbitrary"),
                                             vmem_limit_bytes=VMEM_LIMIT),
        name="deltanet",
    )(qkvb, qkvb, qkvb, qkvb, qkvb, qkvb, bg, bg, conv_w8, tri_l, tri_u)


def _out_kernel(x_ref, oa_ref, of_ref, ob_ref, szb_ref, gate_ref, ng_ref, wa_ref, wb_ref, wo_ref, y_ref):
    o = of_ref[...].astype(F32) + ob_ref[...].astype(F32)
    parts = []
    for h in range(DN_HEADS):
        oh = o[:, h * DN_HEAD_DIM:(h + 1) * DN_HEAD_DIM]
        ms = jnp.mean(oh * oh, axis=-1, keepdims=True)
        parts.append(oh * lax.rsqrt(ms + EPS) * ng_ref[...])
    on = jnp.concatenate(parts, axis=1)
    o_b = (on * szb_ref[...].astype(F32)).astype(BF16)
    ya = _dot(oa_ref[...], wa_ref[...])
    yb = _dot(o_b, wb_ref[...])
    merged = (gate_ref[:, :D_MODEL].astype(F32) * ya + gate_ref[:, D_MODEL:].astype(F32) * yb).astype(BF16)
    y_ref[...] = x_ref[...] + _dot(merged, wo_ref[...])


def _out_stage(x2, oa, of, ob, szb, gates, ng, wa, wb, wo):
    n = x2.shape[0]
    tm = TM_PROJ
    tok = lambda width: pl.BlockSpec((tm, width), lambda i: (i, 0))
    return pl.pallas_call(
        _out_kernel,
        grid=(n // tm,),
        in_specs=[tok(D_MODEL), tok(NA_WIDTH), tok(DN_WIDTH), tok(DN_WIDTH), tok(DN_WIDTH), tok(2 * D_MODEL),
                  _const_spec((1, DN_HEAD_DIM)), _const_spec((NA_WIDTH, D_MODEL)),
                  _const_spec((DN_WIDTH, D_MODEL)), _const_spec((D_MODEL, D_MODEL))],
        out_specs=tok(D_MODEL),
        out_shape=jax.ShapeDtypeStruct((n, D_MODEL), F32),
        compiler_params=pltpu.CompilerParams(dimension_semantics=("arbitrary",),
                                             vmem_limit_bytes=VMEM_LIMIT),
        name="out_stage",
    )(x2, oa, of, ob, szb, gates, ng, wa, wb, wo)


def _layer_params(l, norm_g, w_in, attn_q_norm_g, attn_k_norm_g, attn_rpb, dn_conv_w, dn_a_log, dn_dt_bias,
                  dn_norm_g, w_branch_a, w_branch_b, w_out):
    n_ba = 2 * N_DIR * DN_HEADS
    w = w_in[l]
    c_ba = 3 * NA_WIDTH + NA_WIDTH + 3 * DN_WIDTH + DN_WIDTH
    w_re = jnp.concatenate(
        [w[:, :c_ba], w[:, c_ba + n_ba:], w[:, c_ba:c_ba + n_ba],
         jnp.zeros((D_MODEL, LANES - n_ba), w.dtype)], axis=1).astype(BF16)
    pad = lambda vec, lo: jnp.zeros((1, LANES), F32).at[0, lo:lo + vec.size].set(vec.reshape(-1).astype(F32))
    head_id = jnp.arange(NA_WIDTH) // NA_HEAD_DIM
    tok_id = jnp.arange(DN_BLOCK)
    same_chunk = (tok_id[:, None] // CHUNK) == (tok_id[None, :] // CHUNK)
    return dict(
        norm_g=norm_g[l].reshape(1, D_MODEL),
        w_re=w_re,
        gsum=(head_id[:, None] == head_id[None, :]).astype(BF16),
        qg=(jnp.tile(attn_q_norm_g[l], NA_HEADS) * NA_HEAD_DIM ** -0.5).reshape(1, NA_WIDTH),
        kg=jnp.tile(attn_k_norm_g[l], NA_HEADS).reshape(1, NA_WIDTH),
        alog=pad(dn_a_log[l], N_DIR * DN_HEADS),
        dt=pad(dn_dt_bias[l], N_DIR * DN_HEADS),
        bias=_attn_bias_table(attn_rpb[l]),
        conv_w8=jnp.concatenate([dn_conv_w[l].astype(F32), jnp.zeros((8 - CONV_K, 3 * DN_WIDTH), F32)], axis=0),
        tri_l=(same_chunk & (tok_id[:, None] >= tok_id[None, :])).astype(BF16),
        tri_u=(same_chunk & (tok_id[:, None] <= tok_id[None, :])).astype(BF16),
        dn_g=dn_norm_g[l].reshape(1, DN_HEAD_DIM),
        wa=w_branch_a[l].astype(BF16),
        wb=w_branch_b[l].astype(BF16),
        wo=w_out[l].astype(BF16),
    )


def _layer(x, p):
    b, t, _ = x.shape
    x2 = x.reshape(b * t, D_MODEL)
    q, k, v, sza, qkvb, szb, gates, bg = _in_proj(x2, p["norm_g"], p["w_re"], p["gsum"], p["qg"], p["kg"],
                                                  p["alog"], p["dt"])
    r3 = lambda a: a.reshape(b, t, a.shape[-1])
    oa = _attention(r3(q), r3(k), r3(v), r3(sza), p["bias"])
    of, ob = _deltanet(r3(qkvb), r3(bg), p["conv_w8"], p["tri_l"], p["tri_u"])
    y = _out_stage(x2, oa.reshape(b * t, NA_WIDTH), of.reshape(b * t, DN_WIDTH), ob.reshape(b * t, DN_WIDTH),
                   szb, gates, p["dn_g"], p["wa"], p["wb"], p["wo"])
    return y.reshape(b, t, D_MODEL)


def kernel(x_prompt, x_sample, norm_g, w_in, attn_q_norm_g, attn_k_norm_g, attn_rpb, dn_conv_w, dn_a_log,
           dn_dt_bias, dn_norm_g, w_branch_a, w_branch_b, w_out):
    depth = w_in.shape[0]
    params = [_layer_params(l, norm_g, w_in, attn_q_norm_g, attn_k_norm_g, attn_rpb, dn_conv_w, dn_a_log,
                            dn_dt_bias, dn_norm_g, w_branch_a, w_branch_b, w_out) for l in range(depth)]
    outs = []
    for x in (x_prompt, x_sample):
        for p in params:
            x = _layer(x, p)
        outs.append(x)
    return tuple(outs)
```

```python
import functools

import jax
import jax.numpy as jnp
from jax import lax
from jax.experimental import pallas as pl
from jax.experimental.pallas import tpu as pltpu

D_MODEL = 1024
GRID_W = 64
NA_HEADS = 8
NA_HEAD_DIM = 64
NA_WIDTH = NA_HEADS * NA_HEAD_DIM
WIN_R = 8
WIN_C = 16
DN_HEADS = 4
DN_HEAD_DIM = 128
DN_WIDTH = DN_HEADS * DN_HEAD_DIM
CONV_K = 5
CHUNK = 64
N_DIR = 2
EPS = 1e-6

LANES = 128
HALO = 16
MASK_VALUE = -1e30

C_QKV_A = 0
C_Z_A = 3 * NA_WIDTH
C_QKV_B = C_Z_A + NA_WIDTH
C_Z_B = C_QKV_B + 3 * DN_WIDTH
C_GATE = C_Z_B + DN_WIDTH
C_BG = C_GATE + 2 * D_MODEL
C_END = C_BG + LANES

TM_PROJ = 256
ATTN_ROWS = 8
DN_BLOCK = 256
VMEM_LIMIT = 48 * 1024 * 1024

F32 = jnp.float32
BF16 = jnp.bfloat16


def _const_spec(shape):
    return pl.BlockSpec(shape, lambda *_: (0,) * len(shape), pipeline_mode=pl.Buffered(1))


def _sigmoid(x):
    return 1.0 / (1.0 + jnp.exp(-x))


def _silu(x):
    return x * _sigmoid(x)


def _dot(a, b):
    return jnp.dot(a, b, preferred_element_type=F32)


def _dot_nt(a, b):
    return lax.dot_general(a, b, (((1,), (1,)), ((), ())), preferred_element_type=F32)


def _dot_tn(a, b):
    return lax.dot_general(a, b, (((0,), (0,)), ((), ())), preferred_element_type=F32)


def _in_proj_kernel(x_ref, xp_ref, xn_ref, g_ref, w_ref, gs_ref, qg_ref, kg_ref, alog_ref, dt_ref, cw_ref,
                    q_ref, k_ref, v_ref, sza_ref, qkvc_ref, szb_ref, gate_ref, bg_ref, pb_ref, *, tiles_per_seq):
    i = pl.program_id(0)
    tm = TM_PROJ
    x = jnp.concatenate([xp_ref[...], x_ref[...], xn_ref[...]], axis=0)
    ms = jnp.mean(x * x, axis=-1, keepdims=True)
    h_ext = (x * lax.rsqrt(ms + EPS) * g_ref[...]).astype(BF16)
    h = h_ext[HALO:HALO + tm]
    gs = gs_ref[...]

    def proj(lo, hi):
        return _dot(h, w_ref[:, lo:hi])

    def head_norm(y, gain):
        y2 = (y * y).astype(BF16)
        half = NA_WIDTH // 2
        ss = jnp.concatenate([_dot(y2[:, :half], gs), _dot(y2[:, half:], gs)], axis=1)
        return y * lax.rsqrt(ss * (1.0 / NA_HEAD_DIM) + EPS) * gain

    row = lax.broadcasted_iota(jnp.int32, (tm + 2 * HALO, 1), 0)
    seq_tile = i % tiles_per_seq
    valid = ((row >= HALO) | (seq_tile > 0)) & ((row < HALO + tm) | (seq_tile < tiles_per_seq - 1))
    for part in range(3):
        cols = slice(part * DN_WIDTH, (part + 1) * DN_WIDTH)
        lo = C_QKV_B + part * DN_WIDTH
        pb_ref[:, cols] = jnp.where(valid, _dot(h_ext, w_ref[:, lo:lo + DN_WIDTH]), 0.0)

    def conv_slab(slab):
        cols = slice(slab * DN_HEAD_DIM, (slab + 1) * DN_HEAD_DIM)
        acc = None
        for j in range(CONV_K):
            start = HALO - CONV_K // 2 + j
            term = pb_ref[start:start + tm, cols] * cw_ref[j:j + 1, cols]
            acc = term if acc is None else acc + term
        y = _silu(acc)
        if slab < 2 * DN_HEADS:
            y = y * lax.rsqrt(jnp.sum(y * y, axis=-1, keepdims=True) + EPS)
        if slab < DN_HEADS:
            y = y * (DN_HEAD_DIM ** -0.5)
        qkvc_ref[:, cols] = y.astype(BF16)

    slabs = iter(range(3 * DN_HEADS))

    def conv_some(count):
        for _ in range(count):
            conv_slab(next(slabs))

    q_ref[...] = head_norm(proj(C_QKV_A, C_QKV_A + NA_WIDTH), qg_ref[...]).astype(BF16)
    conv_some(2)
    k_ref[...] = head_norm(proj(C_QKV_A + NA_WIDTH, C_QKV_A + 2 * NA_WIDTH), kg_ref[...]).astype(BF16)
    conv_some(2)
    v_ref[...] = proj(C_QKV_A + 2 * NA_WIDTH, C_Z_A).astype(BF16)
    conv_some(1)
    sza_ref[...] = _silu(proj(C_Z_A, C_QKV_B)).astype(BF16)
    conv_some(1)
    szb_ref[...] = _silu(proj(C_Z_B, C_GATE)).astype(BF16)
    conv_some(1)
    for part in range(4):
        lo = C_GATE + part * DN_WIDTH
        gate_ref[:, part * DN_WIDTH:(part + 1) * DN_WIDTH] = _sigmoid(proj(lo, lo + DN_WIDTH)).astype(BF16)
        conv_some(1)
    ba = proj(C_BG, C_END)
    lane = lax.broadcasted_iota(jnp.int32, ba.shape, 1)
    z = ba + dt_ref[...]
    sp = jnp.maximum(z, 0.0) + jnp.log1p(jnp.exp(-jnp.abs(z)))
    g = -jnp.exp(alog_ref[...]) * sp
    is_g = (lane >= N_DIR * DN_HEADS) & (lane < 2 * N_DIR * DN_HEADS)
    bg_ref[...] = jnp.where(lane < N_DIR * DN_HEADS, _sigmoid(ba), jnp.where(is_g, g, 0.0))
    conv_some(1)


def _in_proj(x2, seq_len, norm_g, w_re, gsum, qg, kg, alog_vec, dt_vec, conv_w8):
    n = x2.shape[0]
    tm = TM_PROJ
    hpt = tm // HALO
    n_halo = n // HALO
    tok = lambda width: pl.BlockSpec((tm, width), lambda i: (i, 0))
    out_shapes = (
        jax.ShapeDtypeStruct((n, NA_WIDTH), BF16),
        jax.ShapeDtypeStruct((n, NA_WIDTH), BF16),
        jax.ShapeDtypeStruct((n, NA_WIDTH), BF16),
        jax.ShapeDtypeStruct((n, NA_WIDTH), BF16),
        jax.ShapeDtypeStruct((n, 3 * DN_WIDTH), BF16),
        jax.ShapeDtypeStruct((n, DN_WIDTH), BF16),
        jax.ShapeDtypeStruct((n, 2 * D_MODEL), BF16),
        jax.ShapeDtypeStruct((n, LANES), F32),
    )
    return pl.pallas_call(
        functools.partial(_in_proj_kernel, tiles_per_seq=seq_len // tm),
        grid=(n // tm,),
        in_specs=[tok(D_MODEL),
                  pl.BlockSpec((HALO, D_MODEL), lambda i: (jnp.maximum(i * hpt - 1, 0), 0)),
                  pl.BlockSpec((HALO, D_MODEL), lambda i: (jnp.minimum((i + 1) * hpt, n_halo - 1), 0)),
                  _const_spec((1, D_MODEL)), _const_spec((D_MODEL, C_END)),
                  _const_spec((NA_WIDTH // 2, NA_WIDTH // 2)), _const_spec((1, NA_WIDTH)),
                  _const_spec((1, NA_WIDTH)),
                  _const_spec((1, LANES)), _const_spec((1, LANES)), _const_spec((8, 3 * DN_WIDTH))],
        out_specs=[tok(NA_WIDTH), tok(NA_WIDTH), tok(NA_WIDTH), tok(NA_WIDTH), tok(3 * DN_WIDTH),
                   tok(DN_WIDTH), tok(2 * D_MODEL), tok(LANES)],
        out_shape=out_shapes,
        scratch_shapes=[pltpu.VMEM((tm + 2 * HALO, 3 * DN_WIDTH), F32)],
        compiler_params=pltpu.CompilerParams(dimension_semantics=("arbitrary",),
                                             vmem_limit_bytes=VMEM_LIMIT),
        name="in_proj",
    )(x2, x2, x2, norm_g, w_re, gsum, qg, kg, alog_vec, dt_vec, conv_w8)


def _attn_kernel(q_ref, k_ref, v_ref, sza_ref, bias_ref, o_ref, *, rows):
    blk = pl.program_id(1)
    lane = lax.broadcasted_iota(jnp.int32, (GRID_W, LANES), 1)
    lo_half = lane < NA_HEAD_DIM
    n_keys = WIN_R * GRID_W
    n_pairs = NA_HEADS // 2
    ones = jnp.ones((n_keys, LANES), BF16)

    def row_body(rl, carry):
        r = blk * ATTN_ROWS + rl
        rs = jnp.clip(r - WIN_R // 2, 0, rows - WIN_R)
        var = r - rs
        kstart = pl.multiple_of(rs * GRID_W, GRID_W)
        qstart = pl.multiple_of(rl * GRID_W, GRID_W)
        st = []
        for p in range(n_pairs):
            cols = slice(p * LANES, (p + 1) * LANES)
            q2 = q_ref[0, pl.ds(qstart, GRID_W), cols]
            zero = jnp.zeros_like(q2)
            qm = jnp.concatenate([jnp.where(lo_half, q2, zero), jnp.where(lo_half, zero, q2)], axis=0)
            k2 = k_ref[0, pl.ds(kstart, n_keys), cols]
            st.append(_dot_nt(k2, qm) + bias_ref[p, var])
        mx = [jnp.max(s, axis=0, keepdims=True) for s in st]
        pe = [jnp.exp(s - m).astype(BF16) for s, m in zip(st, mx)]
        for p in range(n_pairs):
            cols = slice(p * LANES, (p + 1) * LANES)
            v2 = v_ref[0, pl.ds(kstart, n_keys), cols]
            ox = _dot_tn(pe[p], jnp.concatenate([v2, ones], axis=1))
            on = ox[:, :LANES] / ox[:, LANES:]
            o2 = jnp.where(lo_half, on[:GRID_W], on[GRID_W:])
            sz = sza_ref[0, pl.ds(qstart, GRID_W), cols].astype(F32)
            o_ref[0, pl.ds(qstart, GRID_W), cols] = (o2 * sz).astype(BF16)
        return carry

    lax.fori_loop(0, ATTN_ROWS, row_body, 0)


def _attention(q, k, v, sza, bias):
    b, t, _ = q.shape
    rows = t // GRID_W
    assert rows >= WIN_R and rows % ATTN_ROWS == 0
    tq = ATTN_ROWS * GRID_W
    blk = pl.BlockSpec((1, tq, NA_WIDTH), lambda bi, i: (bi, i, 0))
    full = pl.BlockSpec((1, t, NA_WIDTH), lambda bi, i: (bi, 0, 0), pipeline_mode=pl.Buffered(1))
    return pl.pallas_call(
        functools.partial(_attn_kernel, rows=rows),
        grid=(b, rows // ATTN_ROWS),
        in_specs=[blk, full, full, blk, _const_spec(bias.shape)],
        out_specs=blk,
        out_shape=jax.ShapeDtypeStruct((b, t, NA_WIDTH), BF16),
        compiler_params=pltpu.CompilerParams(dimension_semantics=("arbitrary", "arbitrary"),
                                             vmem_limit_bytes=VMEM_LIMIT),
        name="attn",
    )(q, k, v, sza, bias)


def _attn_bias_table(rpb):
    n_dr = 2 * WIN_R - 1
    n_dc = 2 * WIN_C - 1
    qc = jnp.arange(GRID_W)
    kc = jnp.arange(GRID_W)
    col_start = jnp.clip(qc - WIN_C // 2, 0, GRID_W - WIN_C)
    ok = (kc[None, :] >= col_start[:, None]) & (kc[None, :] < col_start[:, None] + WIN_C)
    period = 2 * GRID_W
    lead = GRID_W - WIN_C
    ext = jnp.pad(rpb.astype(F32), ((0, 0), (0, 0), (lead, period - lead - n_dc)))
    flat = jnp.tile(ext, (1, 1, GRID_W))[:, :, :GRID_W * (period - 1)]
    band = flat.reshape(NA_HEADS, n_dr, GRID_W, period - 1)[:, :, :, GRID_W - 1:]
    band = jnp.where(ok[None, None], band, MASK_VALUE)
    tab = jnp.stack([band[:, WIN_R - 1 - var:2 * WIN_R - 1 - var] for var in range(WIN_R)], axis=1)
    tab = tab.reshape(NA_HEADS // 2, 2, WIN_R, WIN_R, GRID_W, GRID_W)
    tab = jnp.transpose(tab, (0, 2, 3, 5, 1, 4))
    return tab.reshape(NA_HEADS // 2, WIN_R, WIN_R * GRID_W, 2 * GRID_W)


def _split3_dot(tri, x):
    x1 = x.astype(BF16)
    r1 = x - x1.astype(F32)
    x2 = r1.astype(BF16)
    x3 = (r1 - x2.astype(F32)).astype(BF16)
    return _dot(tri, x1) + _dot(tri, x2) + _dot(tri, x3)


def _dn_kernel(xf_ref, xb_ref, bgf_ref, bgb_ref, tl_ref, tu_ref, of_ref, ob_ref, s_ref):
    i = pl.program_id(1)
    tb = DN_BLOCK
    nc = tb // CHUNK

    @pl.when(i == 0)
    def _():
        s_ref[...] = jnp.zeros_like(s_ref)

    ii = lax.broadcasted_iota(jnp.int32, (tb, tb), 0)
    jj = lax.broadcasted_iota(jnp.int32, (tb, tb), 1)
    same = (ii // CHUNK) == (jj // CHUNK)
    eye = (ii == jj).astype(F32)
    lane = lax.broadcasted_iota(jnp.int32, (tb, LANES), 1)
    is_g = (lane >= N_DIR * DN_HEADS) & (lane < 2 * N_DIR * DN_HEADS)

    chains = []
    for d in range(N_DIR):
        x_ref = xf_ref if d == 0 else xb_ref
        bg_ref = bgf_ref if d == 0 else bgb_ref
        incl = same & ((ii >= jj) if d == 0 else (ii <= jj))
        strict = same & ((ii > jj) if d == 0 else (ii < jj))

        bg = bg_ref[0]
        gvals = jnp.where(is_g, bg, 0.0)
        gc = _split3_dot(tl_ref[...] if d == 0 else tu_ref[...], gvals)
        gct = gc.T
        eg = jnp.exp(gc)

        for h in range(DN_HEADS):
            bi = d * DN_HEADS + h
            ci = N_DIR * DN_HEADS + bi
            q_bf = x_ref[0, :, h * DN_HEAD_DIM:(h + 1) * DN_HEAD_DIM]
            k_bf = x_ref[0, :, DN_WIDTH + h * DN_HEAD_DIM:DN_WIDTH + (h + 1) * DN_HEAD_DIM]
            v_bf = x_ref[0, :, 2 * DN_WIDTH + h * DN_HEAD_DIM:2 * DN_WIDTH + (h + 1) * DN_HEAD_DIM]
            kh = k_bf.astype(F32)
            beta = bg[:, bi:bi + 1]
            gcol_all = gc[:, ci:ci + 1]
            egcol = eg[:, ci:ci + 1]
            rhs = jnp.concatenate([kh * (beta * egcol), v_bf.astype(F32) * beta], axis=1).astype(BF16)
            kk = _dot_nt(k_bf, k_bf)
            qk = _dot_nt(q_bf, k_bf)
            diff = gcol_all - gct[ci:ci + 1, :]
            decay = jnp.where(incl, jnp.exp(jnp.where(incl, diff, 0.0)), 0.0)
            glast, kdec = [], []
            for c in range(nc):
                rc = slice(c * CHUNK, (c + 1) * CHUNK)
                row = (c + 1) * CHUNK - 1 if d == 0 else c * CHUNK
                gl = gcol_all[row:row + 1]
                glast.append(gl)
                kdec.append((kh[rc] * jnp.exp(gl - gcol_all[rc])).astype(BF16))
            chains.append(dict(
                d=d, h=h, bi=bi,
                m=jnp.where(strict, -(kk * beta * decay), 0.0),
                intra=(qk * decay).astype(BF16),
                rhs=rhs, qdec=q_bf.astype(F32) * egcol, glast=glast, kdec=kdec))

    xs = [ch["m"].astype(BF16) for ch in chains]
    ps = [eye + ch["m"] for ch in chains]
    for _ in range(5):
        xs = [_dot(x, x).astype(BF16) for x in xs]
        ps = [p + _dot(p.astype(BF16), x) for p, x in zip(ps, xs)]

    for ch, p in zip(chains, ps):
        wu = _dot(p.astype(BF16), ch["rhs"]).astype(BF16)
        iw = _dot(ch["intra"], wu)
        ch["qp"] = (ch["qdec"] - iw[:, :DN_HEAD_DIM]).astype(BF16)
        ch["oi"] = iw[:, DN_HEAD_DIM:]
        ch["gh"] = [_dot_tn(ch["kdec"][c], wu[c * CHUNK:(c + 1) * CHUNK]) for c in range(nc)]
        ch["s"] = s_ref[ch["bi"]]

    for step in range(nc):
        for ch in chains:
            c = step if ch["d"] == 0 else nc - 1 - step
            rc = slice(c * CHUNK, (c + 1) * CHUNK)
            o_ref = of_ref if ch["d"] == 0 else ob_ref
            s = ch["s"]
            sb = s.astype(BF16)
            o = _dot(ch["qp"][rc], sb) + ch["oi"][rc]
            hs = slice(ch["h"] * DN_HEAD_DIM, (ch["h"] + 1) * DN_HEAD_DIM)
            o_ref[0, rc, hs] = o.astype(o_ref.dtype)
            gh = ch["gh"][c]
            ch["s"] = (s * jnp.exp(ch["glast"][c]) + gh[:, DN_HEAD_DIM:]
                       - _dot(gh[:, :DN_HEAD_DIM].astype(BF16), sb))

    for ch in chains:
        s_ref[ch["bi"]] = ch["s"]


def _deltanet(qkvc, bg, tri_l, tri_u):
    b, t, _ = qkvc.shape
    tb = DN_BLOCK
    nblk = t // tb
    c3 = 3 * DN_WIDTH
    fwd = lambda bi, i: (bi, i, 0)
    bwd = lambda bi, i: (bi, nblk - 1 - i, 0)
    return pl.pallas_call(
        _dn_kernel,
        grid=(b, nblk),
        in_specs=[pl.BlockSpec((1, tb, c3), fwd), pl.BlockSpec((1, tb, c3), bwd),
                  pl.BlockSpec((1, tb, LANES), fwd), pl.BlockSpec((1, tb, LANES), bwd),
                  _const_spec((tb, tb)), _const_spec((tb, tb))],
        out_specs=[pl.BlockSpec((1, tb, DN_WIDTH), fwd), pl.BlockSpec((1, tb, DN_WIDTH), bwd)],
        out_shape=(jax.ShapeDtypeStruct((b, t, DN_WIDTH), BF16),
                   jax.ShapeDtypeStruct((b, t, DN_WIDTH), BF16)),
        scratch_shapes=[pltpu.VMEM((N_DIR * DN_HEADS, DN_HEAD_DIM, DN_HEAD_DIM), F32)],
        compiler_params=pltpu.CompilerParams(dimension_semantics=("arbitrary", "arbitrary"),
                                             vmem_limit_bytes=VMEM_LIMIT),
        name="deltanet",
    )(qkvc, qkvc, bg, bg, tri_l, tri_u)


def _out_kernel(x_ref, oa_ref, of_ref, ob_ref, szb_ref, gate_ref, ng_ref, wa_ref, wb_ref, wo_ref, y_ref):
    o = of_ref[...].astype(F32) + ob_ref[...].astype(F32)
    parts = []
    for h in range(DN_HEADS):
        oh = o[:, h * DN_HEAD_DIM:(h + 1) * DN_HEAD_DIM]
        ms = jnp.mean(oh * oh, axis=-1, keepdims=True)
        parts.append(oh * lax.rsqrt(ms + EPS) * ng_ref[...])
    on = jnp.concatenate(parts, axis=1)
    o_b = (on * szb_ref[...].astype(F32)).astype(BF16)
    ya = _dot(oa_ref[...], wa_ref[...])
    yb = _dot(o_b, wb_ref[...])
    merged = (gate_ref[:, :D_MODEL].astype(F32) * ya + gate_ref[:, D_MODEL:].astype(F32) * yb).astype(BF16)
    y_ref[...] = x_ref[...] + _dot(merged, wo_ref[...])


def _out_stage(x2, oa, of, ob, szb, gates, ng, wa, wb, wo):
    n = x2.shape[0]
    tm = TM_PROJ
    tok = lambda width: pl.BlockSpec((tm, width), lambda i: (i, 0))
    return pl.pallas_call(
        _out_kernel,
        grid=(n // tm,),
        in_specs=[tok(D_MODEL), tok(NA_WIDTH), tok(DN_WIDTH), tok(DN_WIDTH), tok(DN_WIDTH), tok(2 * D_MODEL),
                  _const_spec((1, DN_HEAD_DIM)), _const_spec((NA_WIDTH, D_MODEL)),
                  _const_spec((DN_WIDTH, D_MODEL)), _const_spec((D_MODEL, D_MODEL))],
        out_specs=tok(D_MODEL),
        out_shape=jax.ShapeDtypeStruct((n, D_MODEL), F32),
        compiler_params=pltpu.CompilerParams(dimension_semantics=("arbitrary",),
                                             vmem_limit_bytes=VMEM_LIMIT),
        name="out_stage",
    )(x2, oa, of, ob, szb, gates, ng, wa, wb, wo)


def _layer_params(l, norm_g, w_in, attn_q_norm_g, attn_k_norm_g, attn_rpb, dn_conv_w, dn_a_log, dn_dt_bias,
                  dn_norm_g, w_branch_a, w_branch_b, w_out):
    n_ba = 2 * N_DIR * DN_HEADS
    w = w_in[l]
    c_ba = 3 * NA_WIDTH + NA_WIDTH + 3 * DN_WIDTH + DN_WIDTH
    w_re = jnp.concatenate(
        [w[:, :c_ba], w[:, c_ba + n_ba:], w[:, c_ba:c_ba + n_ba],
         jnp.zeros((D_MODEL, LANES - n_ba), w.dtype)], axis=1).astype(BF16)
    pad = lambda vec, lo: jnp.zeros((1, LANES), F32).at[0, lo:lo + vec.size].set(vec.reshape(-1).astype(F32))
    head_id = jnp.arange(NA_WIDTH // 2) // NA_HEAD_DIM
    tok_id = jnp.arange(DN_BLOCK)
    same_chunk = (tok_id[:, None] // CHUNK) == (tok_id[None, :] // CHUNK)
    return dict(
        norm_g=norm_g[l].reshape(1, D_MODEL),
        w_re=w_re,
        gsum=(head_id[:, None] == head_id[None, :]).astype(BF16),
        qg=(jnp.tile(attn_q_norm_g[l], NA_HEADS) * NA_HEAD_DIM ** -0.5).reshape(1, NA_WIDTH),
        kg=jnp.tile(attn_k_norm_g[l], NA_HEADS).reshape(1, NA_WIDTH),
        alog=pad(dn_a_log[l], N_DIR * DN_HEADS),
        dt=pad(dn_dt_bias[l], N_DIR * DN_HEADS),
        bias=_attn_bias_table(attn_rpb[l]),
        conv_w8=jnp.concatenate([dn_conv_w[l].astype(F32), jnp.zeros((8 - CONV_K, 3 * DN_WIDTH), F32)], axis=0),
        tri_l=(same_chunk & (tok_id[:, None] >= tok_id[None, :])).astype(BF16),
        tri_u=(same_chunk & (tok_id[:, None] <= tok_id[None, :])).astype(BF16),
        dn_g=dn_norm_g[l].reshape(1, DN_HEAD_DIM),
        wa=w_branch_a[l].astype(BF16),
        wb=w_branch_b[l].astype(BF16),
        wo=w_out[l].astype(BF16),
    )


def _layer(x, p):
    b, t, _ = x.shape
    assert t % TM_PROJ == 0 and t % DN_BLOCK == 0
    x2 = x.reshape(b * t, D_MODEL)
    q, k, v, sza, qkvc, szb, gates, bg = _in_proj(x2, t, p["norm_g"], p["w_re"], p["gsum"], p["qg"], p["kg"],
                                                  p["alog"], p["dt"], p["conv_w8"])
    r3 = lambda a: a.reshape(b, t, a.shape[-1])
    oa = _attention(r3(q), r3(k), r3(v), r3(sza), p["bias"])
    of, ob = _deltanet(r3(qkvc), r3(bg), p["tri_l"], p["tri_u"])
    y = _out_stage(x2, oa.reshape(b * t, NA_WIDTH), of.reshape(b * t, DN_WIDTH), ob.reshape(b * t, DN_WIDTH),
                   szb, gates, p["dn_g"], p["wa"], p["wb"], p["wo"])
    return y.reshape(b, t, D_MODEL)


def kernel(x_prompt, x_sample, norm_g, w_in, attn_q_norm_g, attn_k_norm_g, attn_rpb, dn_conv_w, dn_a_log,
           dn_dt_bias, dn_norm_g, w_branch_a, w_branch_b, w_out):
    depth = w_in.shape[0]
    params = [_layer_params(l, norm_g, w_in, attn_q_norm_g, attn_k_norm_g, attn_rpb, dn_conv_w, dn_a_log,
                            dn_dt_bias, dn_norm_g, w_branch_a, w_branch_b, w_out) for l in range(depth)]
    outs = []
    for x in (x_prompt, x_sample):
        for p in params:
            x = _layer(x, p)
        outs.append(x)
    return tuple(outs)
```

```python
import functools

import jax
import jax.numpy as jnp
from jax import lax
from jax.experimental import pallas as pl
from jax.experimental.pallas import tpu as pltpu

D_MODEL = 1024
GRID_W = 64
NA_HEADS = 8
NA_HEAD_DIM = 64
NA_WIDTH = NA_HEADS * NA_HEAD_DIM
WIN_R = 8
WIN_C = 16
DN_HEADS = 4
DN_HEAD_DIM = 128
DN_WIDTH = DN_HEADS * DN_HEAD_DIM
CONV_K = 5
CHUNK = 64
N_DIR = 2
EPS = 1e-6

LANES = 128
HALO = 16
MASK_VALUE = -1e30

C_QKV_A = 0
C_Z_A = 3 * NA_WIDTH
C_QKV_B = C_Z_A + NA_WIDTH
C_Z_B = C_QKV_B + 3 * DN_WIDTH
C_GATE = C_Z_B + DN_WIDTH
C_BG = C_GATE + 2 * D_MODEL
C_END = C_BG + LANES

TM_PROJ = 512
ATTN_ROWS = 8
DN_BLOCK = 256
VMEM_LIMIT = 48 * 1024 * 1024

F32 = jnp.float32
BF16 = jnp.bfloat16


def _const_spec(shape):
    return pl.BlockSpec(shape, lambda *_: (0,) * len(shape), pipeline_mode=pl.Buffered(1))


def _sigmoid(x):
    return 0.5 * jnp.tanh(0.5 * x) + 0.5


def _silu(x):
    return x * _sigmoid(x)


def _dot(a, b):
    return jnp.dot(a, b, preferred_element_type=F32)


def _dot_nt(a, b):
    return lax.dot_general(a, b, (((1,), (1,)), ((), ())), preferred_element_type=F32)


def _dot_tn(a, b):
    return lax.dot_general(a, b, (((0,), (0,)), ((), ())), preferred_element_type=F32)


def _in_proj_kernel(x_ref, xp_ref, xn_ref, g_ref, w_ref, gs_ref, qg_ref, kg_ref, alog_ref, dt_ref, cw_ref,
                    q_ref, k_ref, v_ref, sza_ref, qkvc_ref, szb_ref, gate_ref, bg_ref, pb_ref, *, tiles_per_seq):
    i = pl.program_id(0)
    tm = TM_PROJ
    x = jnp.concatenate([xp_ref[...], x_ref[...], xn_ref[...]], axis=0)
    ms = jnp.mean(x * x, axis=-1, keepdims=True)
    h_ext = (x * lax.rsqrt(ms + EPS) * g_ref[...]).astype(BF16)
    h = h_ext[HALO:HALO + tm]
    gs = gs_ref[...]

    def proj(lo, hi):
        return _dot(h, w_ref[:, lo:hi])

    def head_norm(y, gain):
        y2 = (y * y).astype(BF16)
        half = NA_WIDTH // 2
        ss = jnp.concatenate([_dot(y2[:, :half], gs), _dot(y2[:, half:], gs)], axis=1)
        return y * lax.rsqrt(ss * (1.0 / NA_HEAD_DIM) + EPS) * gain

    row = lax.broadcasted_iota(jnp.int32, (tm + 2 * HALO, 1), 0)
    seq_tile = i % tiles_per_seq
    valid = ((row >= HALO) | (seq_tile > 0)) & ((row < HALO + tm) | (seq_tile < tiles_per_seq - 1))
    for part in range(3):
        cols = slice(part * DN_WIDTH, (part + 1) * DN_WIDTH)
        lo = C_QKV_B + part * DN_WIDTH
        pb_ref[:, cols] = jnp.where(valid, _dot(h_ext, w_ref[:, lo:lo + DN_WIDTH]), 0.0)

    def conv_slab(slab):
        cols = slice(slab * DN_HEAD_DIM, (slab + 1) * DN_HEAD_DIM)
        acc = None
        for j in range(CONV_K):
            start = HALO - CONV_K // 2 + j
            term = pb_ref[start:start + tm, cols] * cw_ref[j:j + 1, cols]
            acc = term if acc is None else acc + term
        y = _silu(acc)
        if slab < 2 * DN_HEADS:
            y = y * lax.rsqrt(jnp.sum(y * y, axis=-1, keepdims=True) + EPS)
        if slab < DN_HEADS:
            y = y * (DN_HEAD_DIM ** -0.5)
        qkvc_ref[:, cols] = y.astype(BF16)

    slabs = iter(range(3 * DN_HEADS))

    def conv_some(count):
        for _ in range(count):
            conv_slab(next(slabs))

    q_ref[...] = head_norm(proj(C_QKV_A, C_QKV_A + NA_WIDTH), qg_ref[...]).astype(BF16)
    conv_some(2)
    k_ref[...] = head_norm(proj(C_QKV_A + NA_WIDTH, C_QKV_A + 2 * NA_WIDTH), kg_ref[...]).astype(BF16)
    conv_some(2)
    v_ref[...] = proj(C_QKV_A + 2 * NA_WIDTH, C_Z_A).astype(BF16)
    conv_some(1)
    sza_ref[...] = _silu(proj(C_Z_A, C_QKV_B)).astype(BF16)
    conv_some(1)
    szb_ref[...] = _silu(proj(C_Z_B, C_GATE)).astype(BF16)
    conv_some(1)
    for part in range(4):
        lo = C_GATE + part * DN_WIDTH
        gate_ref[:, part * DN_WIDTH:(part + 1) * DN_WIDTH] = _sigmoid(proj(lo, lo + DN_WIDTH)).astype(BF16)
        conv_some(1)
    ba = proj(C_BG, C_END)
    lane = lax.broadcasted_iota(jnp.int32, ba.shape, 1)
    z = ba + dt_ref[...]
    sp = jnp.maximum(z, 0.0) + jnp.log1p(jnp.exp(-jnp.abs(z)))
    g = -jnp.exp(alog_ref[...]) * sp
    is_g = (lane >= N_DIR * DN_HEADS) & (lane < 2 * N_DIR * DN_HEADS)
    bg_ref[...] = jnp.where(lane < N_DIR * DN_HEADS, _sigmoid(ba), jnp.where(is_g, g, 0.0))
    conv_some(1)


def _in_proj(x2, seq_len, norm_g, w_re, gsum, qg, kg, alog_vec, dt_vec, conv_w8):
    n = x2.shape[0]
    tm = TM_PROJ
    hpt = tm // HALO
    n_halo = n // HALO
    tok = lambda width: pl.BlockSpec((tm, width), lambda i: (i, 0))
    out_shapes = (
        jax.ShapeDtypeStruct((n, NA_WIDTH), BF16),
        jax.ShapeDtypeStruct((n, NA_WIDTH), BF16),
        jax.ShapeDtypeStruct((n, NA_WIDTH), BF16),
        jax.ShapeDtypeStruct((n, NA_WIDTH), BF16),
        jax.ShapeDtypeStruct((n, 3 * DN_WIDTH), BF16),
        jax.ShapeDtypeStruct((n, DN_WIDTH), BF16),
        jax.ShapeDtypeStruct((n, 2 * D_MODEL), BF16),
        jax.ShapeDtypeStruct((n, LANES), F32),
    )
    return pl.pallas_call(
        functools.partial(_in_proj_kernel, tiles_per_seq=seq_len // tm),
        grid=(n // tm,),
        in_specs=[tok(D_MODEL),
                  pl.BlockSpec((HALO, D_MODEL), lambda i: (jnp.maximum(i * hpt - 1, 0), 0)),
                  pl.BlockSpec((HALO, D_MODEL), lambda i: (jnp.minimum((i + 1) * hpt, n_halo - 1), 0)),
                  _const_spec((1, D_MODEL)), _const_spec((D_MODEL, C_END)),
                  _const_spec((NA_WIDTH // 2, NA_WIDTH // 2)), _const_spec((1, NA_WIDTH)),
                  _const_spec((1, NA_WIDTH)),
                  _const_spec((1, LANES)), _const_spec((1, LANES)), _const_spec((8, 3 * DN_WIDTH))],
        out_specs=[tok(NA_WIDTH), tok(NA_WIDTH), tok(NA_WIDTH), tok(NA_WIDTH), tok(3 * DN_WIDTH),
                   tok(DN_WIDTH), tok(2 * D_MODEL), tok(LANES)],
        out_shape=out_shapes,
        scratch_shapes=[pltpu.VMEM((tm + 2 * HALO, 3 * DN_WIDTH), F32)],
        compiler_params=pltpu.CompilerParams(dimension_semantics=("arbitrary",),
                                             vmem_limit_bytes=VMEM_LIMIT),
        name="in_proj",
    )(x2, x2, x2, norm_g, w_re, gsum, qg, kg, alog_vec, dt_vec, conv_w8)


def _attn_kernel(q_ref, k_ref, v_ref, sza_ref, bias_ref, o_ref, *, rows):
    blk = pl.program_id(1)
    lane = lax.broadcasted_iota(jnp.int32, (GRID_W, LANES), 1)
    lo_half = lane < NA_HEAD_DIM
    n_keys = WIN_R * GRID_W
    n_pairs = NA_HEADS // 2
    ones = jnp.ones((n_keys, LANES), BF16)

    def row_body(rl, carry):
        r = blk * ATTN_ROWS + rl
        rs = jnp.clip(r - WIN_R // 2, 0, rows - WIN_R)
        var = r - rs
        kstart = pl.multiple_of(rs * GRID_W, GRID_W)
        qstart = pl.multiple_of(rl * GRID_W, GRID_W)
        st = []
        for p in range(n_pairs):
            cols = slice(p * LANES, (p + 1) * LANES)
            q2 = q_ref[0, pl.ds(qstart, GRID_W), cols]
            zero = jnp.zeros_like(q2)
            qm = jnp.concatenate([jnp.where(lo_half, q2, zero), jnp.where(lo_half, zero, q2)], axis=0)
            k2 = k_ref[0, pl.ds(kstart, n_keys), cols]
            st.append(_dot_nt(k2, qm) + bias_ref[p, var])
        mx = [jnp.max(s, axis=0, keepdims=True) for s in st]
        pe = [jnp.exp(s - m).astype(BF16) for s, m in zip(st, mx)]
        for p in range(n_pairs):
            cols = slice(p * LANES, (p + 1) * LANES)
            v2 = v_ref[0, pl.ds(kstart, n_keys), cols]
            ox = _dot_tn(pe[p], jnp.concatenate([v2, ones], axis=1))
            on = ox[:, :LANES] / ox[:, LANES:]
            o2 = jnp.where(lo_half, on[:GRID_W], on[GRID_W:])
            sz = sza_ref[0, pl.ds(qstart, GRID_W), cols].astype(F32)
            o_ref[0, pl.ds(qstart, GRID_W), cols] = (o2 * sz).astype(BF16)
        return carry

    lax.fori_loop(0, ATTN_ROWS, row_body, 0, unroll=4)


def _attention(q, k, v, sza, bias):
    b, t, _ = q.shape
    rows = t // GRID_W
    assert rows >= WIN_R and rows % ATTN_ROWS == 0
    tq = ATTN_ROWS * GRID_W
    blk = pl.BlockSpec((1, tq, NA_WIDTH), lambda bi, i: (bi, i, 0))
    full = pl.BlockSpec((1, t, NA_WIDTH), lambda bi, i: (bi, 0, 0), pipeline_mode=pl.Buffered(1))
    return pl.pallas_call(
        functools.partial(_attn_kernel, rows=rows),
        grid=(b, rows // ATTN_ROWS),
        in_specs=[blk, full, full, blk, _const_spec(bias.shape)],
        out_specs=blk,
        out_shape=jax.ShapeDtypeStruct((b, t, NA_WIDTH), BF16),
        compiler_params=pltpu.CompilerParams(dimension_semantics=("arbitrary", "arbitrary"),
                                             vmem_limit_bytes=VMEM_LIMIT),
        name="attn",
    )(q, k, v, sza, bias)


def _attn_bias_table(rpb):
    n_dr = 2 * WIN_R - 1
    n_dc = 2 * WIN_C - 1
    qc = jnp.arange(GRID_W)
    kc = jnp.arange(GRID_W)
    col_start = jnp.clip(qc - WIN_C // 2, 0, GRID_W - WIN_C)
    ok = (kc[None, :] >= col_start[:, None]) & (kc[None, :] < col_start[:, None] + WIN_C)
    period = 2 * GRID_W
    lead = GRID_W - WIN_C
    ext = jnp.pad(rpb.astype(F32), ((0, 0), (0, 0), (lead, period - lead - n_dc)))
    flat = jnp.tile(ext, (1, 1, GRID_W))[:, :, :GRID_W * (period - 1)]
    band = flat.reshape(NA_HEADS, n_dr, GRID_W, period - 1)[:, :, :, GRID_W - 1:]
    band = jnp.where(ok[None, None], band, MASK_VALUE)
    tab = jnp.stack([band[:, WIN_R - 1 - var:2 * WIN_R - 1 - var] for var in range(WIN_R)], axis=1)
    tab = tab.reshape(NA_HEADS // 2, 2, WIN_R, WIN_R, GRID_W, GRID_W)
    tab = jnp.transpose(tab, (0, 2, 3, 5, 1, 4))
    return tab.reshape(NA_HEADS // 2, WIN_R, WIN_R * GRID_W, 2 * GRID_W)


def _split3_dot(tri, x):
    x1 = x.astype(BF16)
    r1 = x - x1.astype(F32)
    x2 = r1.astype(BF16)
    x3 = (r1 - x2.astype(F32)).astype(BF16)
    return _dot(tri, x1) + _dot(tri, x2) + _dot(tri, x3)


def _dn_kernel(xf_ref, xb_ref, bgf_ref, bgb_ref, tl_ref, tu_ref, of_ref, ob_ref, s_ref):
    i = pl.program_id(1)
    tb = DN_BLOCK
    nc = tb // CHUNK

    @pl.when(i == 0)
    def _():
        s_ref[...] = jnp.zeros_like(s_ref)

    half = tb // 2
    ri = lax.broadcasted_iota(jnp.int32, (tb, half), 0) % half
    nn = lax.broadcasted_iota(jnp.int32, (tb, half), 1)
    same = (ri // CHUNK) == (nn // CHUNK)
    eye = (ri == nn).astype(F32)
    lane = lax.broadcasted_iota(jnp.int32, (tb, LANES), 1)
    is_g = (lane >= N_DIR * DN_HEADS) & (lane < 2 * N_DIR * DN_HEADS)

    def block_diag(x):
        z = jnp.zeros((half, half), x.dtype)
        return jnp.concatenate([jnp.concatenate([x[:half], z], axis=1),
                                jnp.concatenate([z, x[half:]], axis=1)], axis=0)

    def gram(a, b):
        return jnp.concatenate([_dot_nt(a[:half], b[:half]), _dot_nt(a[half:], b[half:])], axis=0)

    chains = []
    for d in range(N_DIR):
        x_ref = xf_ref if d == 0 else xb_ref
        bg_ref = bgf_ref if d == 0 else bgb_ref
        incl = same & ((ri >= nn) if d == 0 else (ri <= nn))
        strict = same & ((ri > nn) if d == 0 else (ri < nn))

        bg = bg_ref[0]
        gvals = jnp.where(is_g, bg, 0.0)
        gc = _split3_dot(tl_ref[...] if d == 0 else tu_ref[...], gvals)
        gct = gc.T
        eg = jnp.exp(gc)

        for h in range(DN_HEADS):
            bi = d * DN_HEADS + h
            ci = N_DIR * DN_HEADS + bi
            q_bf = x_ref[0, :, h * DN_HEAD_DIM:(h + 1) * DN_HEAD_DIM]
            k_bf = x_ref[0, :, DN_WIDTH + h * DN_HEAD_DIM:DN_WIDTH + (h + 1) * DN_HEAD_DIM]
            v_bf = x_ref[0, :, 2 * DN_WIDTH + h * DN_HEAD_DIM:2 * DN_WIDTH + (h + 1) * DN_HEAD_DIM]
            kh = k_bf.astype(F32)
            beta = bg[:, bi:bi + 1]
            gcol_all = gc[:, ci:ci + 1]
            egcol = eg[:, ci:ci + 1]
            rhs = jnp.concatenate([kh * (beta * egcol), v_bf.astype(F32) * beta], axis=1).astype(BF16)
            kk = gram(k_bf, k_bf)
            qk = gram(q_bf, k_bf)
            diff = jnp.concatenate([gcol_all[:half] - gct[ci:ci + 1, :half],
                                    gcol_all[half:] - gct[ci:ci + 1, half:]], axis=0)
            decay = jnp.where(incl, jnp.exp(jnp.where(incl, diff, 0.0)), 0.0)
            glast, kdec = [], []
            for c in range(nc):
                rc = slice(c * CHUNK, (c + 1) * CHUNK)
                row = (c + 1) * CHUNK - 1 if d == 0 else c * CHUNK
                gl = gcol_all[row:row + 1]
                glast.append(gl)
                kdec.append((kh[rc] * jnp.exp(gl - gcol_all[rc])).astype(BF16))
            chains.append(dict(
                d=d, h=h, bi=bi,
                m=jnp.where(strict, -(kk * beta * decay), 0.0),
                intra=(qk * decay).astype(BF16),
                rhs=rhs, qdec=q_bf.astype(F32) * egcol, glast=glast, kdec=kdec))

    xs = [ch["m"].astype(BF16) for ch in chains]
    ps = [eye + ch["m"] for ch in chains]
    xs = [_dot(block_diag(x), x).astype(BF16) for x in xs]
    for _ in range(4):
        outs = [_dot(block_diag(x), jnp.concatenate([x, p.astype(BF16)], axis=1)) for x, p in zip(xs, ps)]
        xs = [o[:, :half].astype(BF16) for o in outs]
        ps = [p + o[:, half:] for p, o in zip(ps, outs)]
    ps = [p + _dot(block_diag(x), p.astype(BF16)) for x, p in zip(xs, ps)]

    for ch, p in zip(chains, ps):
        wu = _dot(block_diag(p.astype(BF16)), ch["rhs"]).astype(BF16)
        iw = _dot(block_diag(ch["intra"]), wu)
        ch["qp"] = (ch["qdec"] - iw[:, :DN_HEAD_DIM]).astype(BF16)
        ch["oi"] = iw[:, DN_HEAD_DIM:]
        ch["gh"] = [_dot_tn(ch["kdec"][c], wu[c * CHUNK:(c + 1) * CHUNK]) for c in range(nc)]
        ch["s"] = s_ref[ch["bi"]]

    for step in range(nc):
        for ch in chains:
            c = step if ch["d"] == 0 else nc - 1 - step
            rc = slice(c * CHUNK, (c + 1) * CHUNK)
            o_ref = of_ref if ch["d"] == 0 else ob_ref
            s = ch["s"]
            sb = s.astype(BF16)
            o = _dot(ch["qp"][rc], sb) + ch["oi"][rc]
            hs = slice(ch["h"] * DN_HEAD_DIM, (ch["h"] + 1) * DN_HEAD_DIM)
            o_ref[0, rc, hs] = o.astype(o_ref.dtype)
            gh = ch["gh"][c]
            ch["s"] = (s * jnp.exp(ch["glast"][c]) + gh[:, DN_HEAD_DIM:]
                       - _dot(gh[:, :DN_HEAD_DIM].astype(BF16), sb))

    for ch in chains:
        s_ref[ch["bi"]] = ch["s"]


def _deltanet(qkvc, bg, tri_l, tri_u):
    b, t, _ = qkvc.shape
    tb = DN_BLOCK
    nblk = t // tb
    c3 = 3 * DN_WIDTH
    fwd = lambda bi, i: (bi, i, 0)
    bwd = lambda bi, i: (bi, nblk - 1 - i, 0)
    return pl.pallas_call(
        _dn_kernel,
        grid=(b, nblk),
        in_specs=[pl.BlockSpec((1, tb, c3), fwd), pl.BlockSpec((1, tb, c3), bwd),
                  pl.BlockSpec((1, tb, LANES), fwd), pl.BlockSpec((1, tb, LANES), bwd),
                  _const_spec((tb, tb)), _const_spec((tb, tb))],
        out_specs=[pl.BlockSpec((1, tb, DN_WIDTH), fwd), pl.BlockSpec((1, tb, DN_WIDTH), bwd)],
        out_shape=(jax.ShapeDtypeStruct((b, t, DN_WIDTH), BF16),
                   jax.ShapeDtypeStruct((b, t, DN_WIDTH), BF16)),
        scratch_shapes=[pltpu.VMEM((N_DIR * DN_HEADS, DN_HEAD_DIM, DN_HEAD_DIM), F32)],
        compiler_params=pltpu.CompilerParams(dimension_semantics=("arbitrary", "arbitrary"),
                                             vmem_limit_bytes=VMEM_LIMIT),
        name="deltanet",
    )(qkvc, qkvc, bg, bg, tri_l, tri_u)


def _out_kernel(x_ref, oa_ref, of_ref, ob_ref, szb_ref, gate_ref, ng_ref, wa_ref, wb_ref, wo_ref, y_ref):
    o = of_ref[...].astype(F32) + ob_ref[...].astype(F32)
    parts = []
    for h in range(DN_HEADS):
        oh = o[:, h * DN_HEAD_DIM:(h + 1) * DN_HEAD_DIM]
        ms = jnp.mean(oh * oh, axis=-1, keepdims=True)
        parts.append(oh * lax.rsqrt(ms + EPS) * ng_ref[...])
    on = jnp.concatenate(parts, axis=1)
    o_b = (on * szb_ref[...].astype(F32)).astype(BF16)
    ya = _dot(oa_ref[...], wa_ref[...])
    yb = _dot(o_b, wb_ref[...])
    merged = (gate_ref[:, :D_MODEL].astype(F32) * ya + gate_ref[:, D_MODEL:].astype(F32) * yb).astype(BF16)
    y_ref[...] = x_ref[...] + _dot(merged, wo_ref[...])


def _out_stage(x2, oa, of, ob, szb, gates, ng, wa, wb, wo):
    n = x2.shape[0]
    tm = TM_PROJ
    tok = lambda width: pl.BlockSpec((tm, width), lambda i: (i, 0))
    return pl.pallas_call(
        _out_kernel,
        grid=(n // tm,),
        in_specs=[tok(D_MODEL), tok(NA_WIDTH), tok(DN_WIDTH), tok(DN_WIDTH), tok(DN_WIDTH), tok(2 * D_MODEL),
                  _const_spec((1, DN_HEAD_DIM)), _const_spec((NA_WIDTH, D_MODEL)),
                  _const_spec((DN_WIDTH, D_MODEL)), _const_spec((D_MODEL, D_MODEL))],
        out_specs=tok(D_MODEL),
        out_shape=jax.ShapeDtypeStruct((n, D_MODEL), F32),
        compiler_params=pltpu.CompilerParams(dimension_semantics=("arbitrary",),
                                             vmem_limit_bytes=VMEM_LIMIT),
        name="out_stage",
    )(x2, oa, of, ob, szb, gates, ng, wa, wb, wo)


def _layer_params(l, norm_g, w_in, attn_q_norm_g, attn_k_norm_g, attn_rpb, dn_conv_w, dn_a_log, dn_dt_bias,
                  dn_norm_g, w_branch_a, w_branch_b, w_out):
    n_ba = 2 * N_DIR * DN_HEADS
    w = w_in[l]
    c_ba = 3 * NA_WIDTH + NA_WIDTH + 3 * DN_WIDTH + DN_WIDTH
    w_re = jnp.concatenate(
        [w[:, :c_ba], w[:, c_ba + n_ba:], w[:, c_ba:c_ba + n_ba],
         jnp.zeros((D_MODEL, LANES - n_ba), w.dtype)], axis=1).astype(BF16)
    pad = lambda vec, lo: jnp.zeros((1, LANES), F32).at[0, lo:lo + vec.size].set(vec.reshape(-1).astype(F32))
    head_id = jnp.arange(NA_WIDTH // 2) // NA_HEAD_DIM
    tok_id = jnp.arange(DN_BLOCK)
    same_chunk = (tok_id[:, None] // CHUNK) == (tok_id[None, :] // CHUNK)
    return dict(
        norm_g=norm_g[l].reshape(1, D_MODEL),
        w_re=w_re,
        gsum=(head_id[:, None] == head_id[None, :]).astype(BF16),
        qg=(jnp.tile(attn_q_norm_g[l], NA_HEADS) * NA_HEAD_DIM ** -0.5).reshape(1, NA_WIDTH),
        kg=jnp.tile(attn_k_norm_g[l], NA_HEADS).reshape(1, NA_WIDTH),
        alog=pad(dn_a_log[l], N_DIR * DN_HEADS),
        dt=pad(dn_dt_bias[l], N_DIR * DN_HEADS),
        bias=_attn_bias_table(attn_rpb[l]),
        conv_w8=jnp.concatenate([dn_conv_w[l].astype(F32), jnp.zeros((8 - CONV_K, 3 * DN_WIDTH), F32)], axis=0),
        tri_l=(same_chunk & (tok_id[:, None] >= tok_id[None, :])).astype(BF16),
        tri_u=(same_chunk & (tok_id[:, None] <= tok_id[None, :])).astype(BF16),
        dn_g=dn_norm_g[l].reshape(1, DN_HEAD_DIM),
        wa=w_branch_a[l].astype(BF16),
        wb=w_branch_b[l].astype(BF16),
        wo=w_out[l].astype(BF16),
    )


def _layer(x, p):
    b, t, _ = x.shape
    assert t % TM_PROJ == 0 and t % DN_BLOCK == 0
    x2 = x.reshape(b * t, D_MODEL)
    q, k, v, sza, qkvc, szb, gates, bg = _in_proj(x2, t, p["norm_g"], p["w_re"], p["gsum"], p["qg"], p["kg"],
                                                  p["alog"], p["dt"], p["conv_w8"])
    r3 = lambda a: a.reshape(b, t, a.shape[-1])
    oa = _attention(r3(q), r3(k), r3(v), r3(sza), p["bias"])
    of, ob = _deltanet(r3(qkvc), r3(bg), p["tri_l"], p["tri_u"])
    y = _out_stage(x2, oa.reshape(b * t, NA_WIDTH), of.reshape(b * t, DN_WIDTH), ob.reshape(b * t, DN_WIDTH),
                   szb, gates, p["dn_g"], p["wa"], p["wb"], p["wo"])
    return y.reshape(b, t, D_MODEL)


def kernel(x_prompt, x_sample, norm_g, w_in, attn_q_norm_g, attn_k_norm_g, attn_rpb, dn_conv_w, dn_a_log,
           dn_dt_bias, dn_norm_g, w_branch_a, w_branch_b, w_out):
    depth = w_in.shape[0]
    params = [_layer_params(l, norm_g, w_in, attn_q_norm_g, attn_k_norm_g, attn_rpb, dn_conv_w, dn_a_log,
                            dn_dt_bias, dn_norm_g, w_branch_a, w_branch_b, w_out) for l in range(depth)]
    outs = []
    for x in (x_prompt, x_sample):
        for p in params:
            x = _layer(x, p)
        outs.append(x)
    return tuple(outs)
```

```python
import functools

import jax
import jax.numpy as jnp
from jax import lax
from jax.experimental import pallas as pl
from jax.experimental.pallas import tpu as pltpu

D_MODEL = 1024
GRID_W = 64
NA_HEADS = 8
NA_HEAD_DIM = 64
NA_WIDTH = NA_HEADS * NA_HEAD_DIM
WIN_R = 8
WIN_C = 16
DN_HEADS = 4
DN_HEAD_DIM = 128
DN_WIDTH = DN_HEADS * DN_HEAD_DIM
CONV_K = 5
CHUNK = 64
N_DIR = 2
EPS = 1e-6

LANES = 128
HALO = 16
MASK_VALUE = -1e30

C_QKV_A = 0
C_Z_A = 3 * NA_WIDTH
C_QKV_B = C_Z_A + NA_WIDTH
C_Z_B = C_QKV_B + 3 * DN_WIDTH
C_GATE = C_Z_B + DN_WIDTH
C_BG = C_GATE + 2 * D_MODEL
C_END = C_BG + LANES

TM_PROJ = 512
ATTN_ROWS = 8
DN_BLOCK = 256
VMEM_LIMIT = 48 * 1024 * 1024

F32 = jnp.float32
BF16 = jnp.bfloat16


def _const_spec(shape):
    return pl.BlockSpec(shape, lambda *_: (0,) * len(shape), pipeline_mode=pl.Buffered(1))


def _sigmoid(x):
    return 0.5 * jnp.tanh(0.5 * x) + 0.5


def _silu(x):
    return x * _sigmoid(x)


def _dot(a, b):
    return jnp.dot(a, b, preferred_element_type=F32)


def _dot_nt(a, b):
    return lax.dot_general(a, b, (((1,), (1,)), ((), ())), preferred_element_type=F32)


def _dot_tn(a, b):
    return lax.dot_general(a, b, (((0,), (0,)), ((), ())), preferred_element_type=F32)


def _in_proj_kernel(x_ref, xp_ref, xn_ref, g_ref, w_ref, gs_ref, qg_ref, kg_ref, alog_ref, dt_ref, cw_ref,
                    q_ref, k_ref, v_ref, sza_ref, qkvc_ref, szb_ref, gate_ref, bg_ref, pb_ref, *, tiles_per_seq):
    i = pl.program_id(0)
    tm = TM_PROJ
    x = jnp.concatenate([xp_ref[...], x_ref[...], xn_ref[...]], axis=0)
    ms = jnp.mean(x * x, axis=-1, keepdims=True)
    h_ext = (x * lax.rsqrt(ms + EPS) * g_ref[...]).astype(BF16)
    h = h_ext[HALO:HALO + tm]
    gs = gs_ref[...]

    def proj(lo, hi):
        return _dot(h, w_ref[:, lo:hi])

    def head_norm(y, gain):
        y2 = (y * y).astype(BF16)
        half = NA_WIDTH // 2
        ss = jnp.concatenate([_dot(y2[:, :half], gs), _dot(y2[:, half:], gs)], axis=1)
        return y * lax.rsqrt(ss * (1.0 / NA_HEAD_DIM) + EPS) * gain

    row = lax.broadcasted_iota(jnp.int32, (tm + 2 * HALO, 1), 0)
    seq_tile = i % tiles_per_seq
    valid = ((row >= HALO) | (seq_tile > 0)) & ((row < HALO + tm) | (seq_tile < tiles_per_seq - 1))
    def ordering_zero(y):
        bits = pltpu.bitcast(y[0:8, 0:LANES], jnp.uint32)
        bits = lax.shift_right_logical(lax.shift_right_logical(bits, jnp.uint32(16)), jnp.uint32(16))
        return pltpu.bitcast(bits, F32)[0:1, :]

    def conv_slab(slab, zero_row):
        cols = slice(slab * DN_HEAD_DIM, (slab + 1) * DN_HEAD_DIM)
        acc = zero_row
        for j in range(CONV_K):
            start = HALO - CONV_K // 2 + j
            acc = acc + pb_ref[start:start + tm, cols] * cw_ref[j:j + 1, cols]
        y = _silu(acc)
        if slab < 2 * DN_HEADS:
            y = y * lax.rsqrt(jnp.sum(y * y, axis=-1, keepdims=True) + EPS)
        if slab < DN_HEADS:
            y = y * (DN_HEAD_DIM ** -0.5)
        qkvc_ref[:, cols] = y.astype(BF16)

    slabs = iter(range(3 * DN_HEADS))

    def conv_some(count, after):
        zero_row = ordering_zero(after)
        for _ in range(count):
            conv_slab(next(slabs), zero_row)

    for part in range(3):
        cols = slice(part * DN_WIDTH, (part + 1) * DN_WIDTH)
        lo = C_QKV_B + part * DN_WIDTH
        pr = _dot(h_ext, w_ref[:, lo:lo + DN_WIDTH])
        pb_ref[:, cols] = jnp.where(valid, pr, 0.0)
        if part > 0:
            conv_some(1, pr)
    pr = proj(C_QKV_A, C_QKV_A + NA_WIDTH)
    q_ref[...] = head_norm(pr, qg_ref[...]).astype(BF16)
    conv_some(1, pr)
    pr = proj(C_QKV_A + NA_WIDTH, C_QKV_A + 2 * NA_WIDTH)
    k_ref[...] = head_norm(pr, kg_ref[...]).astype(BF16)
    conv_some(1, pr)
    pr = proj(C_QKV_A + 2 * NA_WIDTH, C_Z_A)
    v_ref[...] = pr.astype(BF16)
    conv_some(1, pr)
    pr = proj(C_Z_A, C_QKV_B)
    sza_ref[...] = _silu(pr).astype(BF16)
    conv_some(1, pr)
    pr = proj(C_Z_B, C_GATE)
    szb_ref[...] = _silu(pr).astype(BF16)
    conv_some(1, pr)
    for part in range(4):
        lo = C_GATE + part * DN_WIDTH
        pr = proj(lo, lo + DN_WIDTH)
        gate_ref[:, part * DN_WIDTH:(part + 1) * DN_WIDTH] = _sigmoid(pr).astype(BF16)
        conv_some(1, pr)
    ba = proj(C_BG, C_END)
    lane = lax.broadcasted_iota(jnp.int32, ba.shape, 1)
    z = ba + dt_ref[...]
    sp = jnp.maximum(z, 0.0) + jnp.log1p(jnp.exp(-jnp.abs(z)))
    g = -jnp.exp(alog_ref[...]) * sp
    is_g = (lane >= N_DIR * DN_HEADS) & (lane < 2 * N_DIR * DN_HEADS)
    bg_ref[...] = jnp.where(lane < N_DIR * DN_HEADS, _sigmoid(ba), jnp.where(is_g, g, 0.0))
    conv_some(1, ba)


def _in_proj(x2, seq_len, norm_g, w_re, gsum, qg, kg, alog_vec, dt_vec, conv_w8):
    n = x2.shape[0]
    tm = TM_PROJ
    hpt = tm // HALO
    n_halo = n // HALO
    tok = lambda width: pl.BlockSpec((tm, width), lambda i: (i, 0))
    out_shapes = (
        jax.ShapeDtypeStruct((n, NA_WIDTH), BF16),
        jax.ShapeDtypeStruct((n, NA_WIDTH), BF16),
        jax.ShapeDtypeStruct((n, NA_WIDTH), BF16),
        jax.ShapeDtypeStruct((n, NA_WIDTH), BF16),
        jax.ShapeDtypeStruct((n, 3 * DN_WIDTH), BF16),
        jax.ShapeDtypeStruct((n, DN_WIDTH), BF16),
        jax.ShapeDtypeStruct((n, 2 * D_MODEL), BF16),
        jax.ShapeDtypeStruct((n, LANES), F32),
    )
    return pl.pallas_call(
        functools.partial(_in_proj_kernel, tiles_per_seq=seq_len // tm),
        grid=(n // tm,),
        in_specs=[tok(D_MODEL),
                  pl.BlockSpec((HALO, D_MODEL), lambda i: (jnp.maximum(i * hpt - 1, 0), 0)),
                  pl.BlockSpec((HALO, D_MODEL), lambda i: (jnp.minimum((i + 1) * hpt, n_halo - 1), 0)),
                  _const_spec((1, D_MODEL)), _const_spec((D_MODEL, C_END)),
                  _const_spec((NA_WIDTH // 2, NA_WIDTH // 2)), _const_spec((1, NA_WIDTH)),
                  _const_spec((1, NA_WIDTH)),
                  _const_spec((1, LANES)), _const_spec((1, LANES)), _const_spec((8, 3 * DN_WIDTH))],
        out_specs=[tok(NA_WIDTH), tok(NA_WIDTH), tok(NA_WIDTH), tok(NA_WIDTH), tok(3 * DN_WIDTH),
                   tok(DN_WIDTH), tok(2 * D_MODEL), tok(LANES)],
        out_shape=out_shapes,
        scratch_shapes=[pltpu.VMEM((tm + 2 * HALO, 3 * DN_WIDTH), F32)],
        compiler_params=pltpu.CompilerParams(dimension_semantics=("arbitrary",),
                                             vmem_limit_bytes=VMEM_LIMIT),
        name="in_proj",
    )(x2, x2, x2, norm_g, w_re, gsum, qg, kg, alog_vec, dt_vec, conv_w8)


def _attn_kernel(q_ref, k_ref, v_ref, sza_ref, bias_ref, o_ref, *, rows):
    blk = pl.program_id(1)
    lane = lax.broadcasted_iota(jnp.int32, (GRID_W, LANES), 1)
    lo_half = lane < NA_HEAD_DIM
    n_keys = WIN_R * GRID_W
    n_pairs = NA_HEADS // 2
    ones = jnp.ones((n_keys, LANES), BF16)

    def row_body(rl, carry):
        r = blk * ATTN_ROWS + rl
        rs = jnp.clip(r - WIN_R // 2, 0, rows - WIN_R)
        var = r - rs
        kstart = pl.multiple_of(rs * GRID_W, GRID_W)
        qstart = pl.multiple_of(rl * GRID_W, GRID_W)
        st = []
        for p in range(n_pairs):
            cols = slice(p * LANES, (p + 1) * LANES)
            q2 = q_ref[0, pl.ds(qstart, GRID_W), cols]
            zero = jnp.zeros_like(q2)
            qm = jnp.concatenate([jnp.where(lo_half, q2, zero), jnp.where(lo_half, zero, q2)], axis=0)
            k2 = k_ref[0, pl.ds(kstart, n_keys), cols]
            st.append(_dot_nt(k2, qm) + bias_ref[p, var])
        mx = [jnp.max(s, axis=0, keepdims=True) for s in st]
        pe = [jnp.exp(s - m).astype(BF16) for s, m in zip(st, mx)]
        for p in range(n_pairs):
            cols = slice(p * LANES, (p + 1) * LANES)
            v2 = v_ref[0, pl.ds(kstart, n_keys), cols]
            ox = _dot_tn(pe[p], jnp.concatenate([v2, ones], axis=1))
            on = ox[:, :LANES] / ox[:, LANES:]
            o2 = jnp.where(lo_half, on[:GRID_W], on[GRID_W:])
            sz = sza_ref[0, pl.ds(qstart, GRID_W), cols].astype(F32)
            o_ref[0, pl.ds(qstart, GRID_W), cols] = (o2 * sz).astype(BF16)
        return carry

    lax.fori_loop(0, ATTN_ROWS, row_body, 0, unroll=8)


def _attention(q, k, v, sza, bias):
    b, t, _ = q.shape
    rows = t // GRID_W
    assert rows >= WIN_R and rows % ATTN_ROWS == 0
    tq = ATTN_ROWS * GRID_W
    blk = pl.BlockSpec((1, tq, NA_WIDTH), lambda bi, i: (bi, i, 0))
    full = pl.BlockSpec((1, t, NA_WIDTH), lambda bi, i: (bi, 0, 0), pipeline_mode=pl.Buffered(1))
    return pl.pallas_call(
        functools.partial(_attn_kernel, rows=rows),
        grid=(b, rows // ATTN_ROWS),
        in_specs=[blk, full, full, blk, _const_spec(bias.shape)],
        out_specs=blk,
        out_shape=jax.ShapeDtypeStruct((b, t, NA_WIDTH), BF16),
        compiler_params=pltpu.CompilerParams(dimension_semantics=("arbitrary", "arbitrary"),
                                             vmem_limit_bytes=VMEM_LIMIT),
        name="attn",
    )(q, k, v, sza, bias)


def _attn_bias_table(rpb):
    n_dr = 2 * WIN_R - 1
    n_dc = 2 * WIN_C - 1
    qc = jnp.arange(GRID_W)
    kc = jnp.arange(GRID_W)
    col_start = jnp.clip(qc - WIN_C // 2, 0, GRID_W - WIN_C)
    ok = (kc[None, :] >= col_start[:, None]) & (kc[None, :] < col_start[:, None] + WIN_C)
    period = 2 * GRID_W
    lead = GRID_W - WIN_C
    ext = jnp.pad(rpb.astype(F32), ((0, 0), (0, 0), (lead, period - lead - n_dc)))
    flat = jnp.tile(ext, (1, 1, GRID_W))[:, :, :GRID_W * (period - 1)]
    band = flat.reshape(NA_HEADS, n_dr, GRID_W, period - 1)[:, :, :, GRID_W - 1:]
    band = jnp.where(ok[None, None], band, MASK_VALUE)
    tab = jnp.stack([band[:, WIN_R - 1 - var:2 * WIN_R - 1 - var] for var in range(WIN_R)], axis=1)
    tab = tab.reshape(NA_HEADS // 2, 2, WIN_R, WIN_R, GRID_W, GRID_W)
    tab = jnp.transpose(tab, (0, 2, 3, 5, 1, 4))
    return tab.reshape(NA_HEADS // 2, WIN_R, WIN_R * GRID_W, 2 * GRID_W)


def _split3_dot(tri, x):
    x1 = x.astype(BF16)
    r1 = x - x1.astype(F32)
    x2 = r1.astype(BF16)
    x3 = (r1 - x2.astype(F32)).astype(BF16)
    return _dot(tri, x1) + _dot(tri, x2) + _dot(tri, x3)


def _dn_kernel(xf_ref, xb_ref, bgf_ref, bgb_ref, tl_ref, tu_ref, of_ref, ob_ref, s_ref):
    i = pl.program_id(1)
    tb = DN_BLOCK
    nc = tb // CHUNK

    @pl.when(i == 0)
    def _():
        s_ref[...] = jnp.zeros_like(s_ref)

    half = tb // 2
    ri = lax.broadcasted_iota(jnp.int32, (tb, half), 0) % half
    nn = lax.broadcasted_iota(jnp.int32, (tb, half), 1)
    same = (ri // CHUNK) == (nn // CHUNK)
    eye = (ri == nn).astype(F32)
    lane = lax.broadcasted_iota(jnp.int32, (tb, LANES), 1)
    is_g = (lane >= N_DIR * DN_HEADS) & (lane < 2 * N_DIR * DN_HEADS)

    def block_diag(x):
        z = jnp.zeros((half, half), x.dtype)
        return jnp.concatenate([jnp.concatenate([x[:half], z], axis=1),
                                jnp.concatenate([z, x[half:]], axis=1)], axis=0)

    def gram(a, b):
        return jnp.concatenate([_dot_nt(a[:half], b[:half]), _dot_nt(a[half:], b[half:])], axis=0)

    chains = []
    for d in range(N_DIR):
        x_ref = xf_ref if d == 0 else xb_ref
        bg_ref = bgf_ref if d == 0 else bgb_ref
        incl = same & ((ri >= nn) if d == 0 else (ri <= nn))
        strict = same & ((ri > nn) if d == 0 else (ri < nn))

        bg = bg_ref[0]
        gvals = jnp.where(is_g, bg, 0.0)
        gc = _split3_dot(tl_ref[...] if d == 0 else tu_ref[...], gvals)
        gct = gc.T
        eg = jnp.exp(gc)

        for h in range(DN_HEADS):
            bi = d * DN_HEADS + h
            ci = N_DIR * DN_HEADS + bi
            q_bf = x_ref[0, :, h * DN_HEAD_DIM:(h + 1) * DN_HEAD_DIM]
            k_bf = x_ref[0, :, DN_WIDTH + h * DN_HEAD_DIM:DN_WIDTH + (h + 1) * DN_HEAD_DIM]
            v_bf = x_ref[0, :, 2 * DN_WIDTH + h * DN_HEAD_DIM:2 * DN_WIDTH + (h + 1) * DN_HEAD_DIM]
            kh = k_bf.astype(F32)
            beta = bg[:, bi:bi + 1]
            gcol_all = gc[:, ci:ci + 1]
            egcol = eg[:, ci:ci + 1]
            rhs = jnp.concatenate([kh * (beta * egcol), v_bf.astype(F32) * beta], axis=1).astype(BF16)
            kk = gram(k_bf, k_bf)
            qk = gram(q_bf, k_bf)
            diff = jnp.concatenate([gcol_all[:half] - gct[ci:ci + 1, :half],
                                    gcol_all[half:] - gct[ci:ci + 1, half:]], axis=0)
            decay = jnp.where(incl, jnp.exp(jnp.where(incl, diff, 0.0)), 0.0)
            glast, kdec = [], []
            for c in range(nc):
                rc = slice(c * CHUNK, (c + 1) * CHUNK)
                row = (c + 1) * CHUNK - 1 if d == 0 else c * CHUNK
                gl = gcol_all[row:row + 1]
                glast.append(gl)
                kdec.append((kh[rc] * jnp.exp(gl - gcol_all[rc])).astype(BF16))
            chains.append(dict(
                d=d, h=h, bi=bi,
                m=jnp.where(strict, -(kk * beta * decay), 0.0),
                intra=(qk * decay).astype(BF16),
                rhs=rhs, qdec=q_bf.astype(F32) * egcol, glast=glast, kdec=kdec))

    xs = [ch["m"].astype(BF16) for ch in chains]
    ps = [eye + ch["m"] for ch in chains]
    xs = [_dot(block_diag(x), x).astype(BF16) for x in xs]
    for _ in range(4):
        outs = [_dot(block_diag(x), jnp.concatenate([x, p.astype(BF16)], axis=1)) for x, p in zip(xs, ps)]
        xs = [o[:, :half].astype(BF16) for o in outs]
        ps = [p + o[:, half:] for p, o in zip(ps, outs)]
    ps = [p + _dot(block_diag(x), p.astype(BF16)) for x, p in zip(xs, ps)]

    for ch, p in zip(chains, ps):
        wu = _dot(block_diag(p.astype(BF16)), ch["rhs"]).astype(BF16)
        iw = _dot(block_diag(ch["intra"]), wu)
        ch["qp"] = (ch["qdec"] - iw[:, :DN_HEAD_DIM]).astype(BF16)
        ch["oi"] = iw[:, DN_HEAD_DIM:]
        ch["gh"] = [_dot_tn(ch["kdec"][c], wu[c * CHUNK:(c + 1) * CHUNK]) for c in range(nc)]
        ch["s"] = s_ref[ch["bi"]]

    for step in range(nc):
        for ch in chains:
            c = step if ch["d"] == 0 else nc - 1 - step
            rc = slice(c * CHUNK, (c + 1) * CHUNK)
            o_ref = of_ref if ch["d"] == 0 else ob_ref
            s = ch["s"]
            sb = s.astype(BF16)
            o = _dot(ch["qp"][rc], sb) + ch["oi"][rc]
            hs = slice(ch["h"] * DN_HEAD_DIM, (ch["h"] + 1) * DN_HEAD_DIM)
            o_ref[0, rc, hs] = o.astype(o_ref.dtype)
            gh = ch["gh"][c]
            ch["s"] = (s * jnp.exp(ch["glast"][c]) + gh[:, DN_HEAD_DIM:]
                       - _dot(gh[:, :DN_HEAD_DIM].astype(BF16), sb))

    for ch in chains:
        s_ref[ch["bi"]] = ch["s"]


def _deltanet(qkvc, bg, tri_l, tri_u):
    b, t, _ = qkvc.shape
    tb = DN_BLOCK
    nblk = t // tb
    c3 = 3 * DN_WIDTH
    fwd = lambda bi, i: (bi, i, 0)
    bwd = lambda bi, i: (bi, nblk - 1 - i, 0)
    return pl.pallas_call(
        _dn_kernel,
        grid=(b, nblk),
        in_specs=[pl.BlockSpec((1, tb, c3), fwd), pl.BlockSpec((1, tb, c3), bwd),
                  pl.BlockSpec((1, tb, LANES), fwd), pl.BlockSpec((1, tb, LANES), bwd),
                  _const_spec((tb, tb)), _const_spec((tb, tb))],
        out_specs=[pl.BlockSpec((1, tb, DN_WIDTH), fwd), pl.BlockSpec((1, tb, DN_WIDTH), bwd)],
        out_shape=(jax.ShapeDtypeStruct((b, t, DN_WIDTH), BF16),
                   jax.ShapeDtypeStruct((b, t, DN_WIDTH), BF16)),
        scratch_shapes=[pltpu.VMEM((N_DIR * DN_HEADS, DN_HEAD_DIM, DN_HEAD_DIM), F32)],
        compiler_params=pltpu.CompilerParams(dimension_semantics=("arbitrary", "arbitrary"),
                                             vmem_limit_bytes=VMEM_LIMIT),
        name="deltanet",
    )(qkvc, qkvc, bg, bg, tri_l, tri_u)


def _out_kernel(x_ref, oa_ref, of_ref, ob_ref, szb_ref, gate_ref, ng_ref, wa_ref, wb_ref, wo_ref, y_ref):
    o = of_ref[...].astype(F32) + ob_ref[...].astype(F32)
    parts = []
    for h in range(DN_HEADS):
        oh = o[:, h * DN_HEAD_DIM:(h + 1) * DN_HEAD_DIM]
        ms = jnp.mean(oh * oh, axis=-1, keepdims=True)
        parts.append(oh * lax.rsqrt(ms + EPS) * ng_ref[...])
    on = jnp.concatenate(parts, axis=1)
    o_b = (on * szb_ref[...].astype(F32)).astype(BF16)
    ya = _dot(oa_ref[...], wa_ref[...])
    yb = _dot(o_b, wb_ref[...])
    merged = (gate_ref[:, :D_MODEL].astype(F32) * ya + gate_ref[:, D_MODEL:].astype(F32) * yb).astype(BF16)
    y_ref[...] = x_ref[...] + _dot(merged, wo_ref[...])


def _out_stage(x2, oa, of, ob, szb, gates, ng, wa, wb, wo):
    n = x2.shape[0]
    tm = TM_PROJ
    tok = lambda width: pl.BlockSpec((tm, width), lambda i: (i, 0))
    return pl.pallas_call(
        _out_kernel,
        grid=(n // tm,),
        in_specs=[tok(D_MODEL), tok(NA_WIDTH), tok(DN_WIDTH), tok(DN_WIDTH), tok(DN_WIDTH), tok(2 * D_MODEL),
                  _const_spec((1, DN_HEAD_DIM)), _const_spec((NA_WIDTH, D_MODEL)),
                  _const_spec((DN_WIDTH, D_MODEL)), _const_spec((D_MODEL, D_MODEL))],
        out_specs=tok(D_MODEL),
        out_shape=jax.ShapeDtypeStruct((n, D_MODEL), F32),
        compiler_params=pltpu.CompilerParams(dimension_semantics=("arbitrary",),
                                             vmem_limit_bytes=VMEM_LIMIT),
        name="out_stage",
    )(x2, oa, of, ob, szb, gates, ng, wa, wb, wo)


def _layer_params(l, norm_g, w_in, attn_q_norm_g, attn_k_norm_g, attn_rpb, dn_conv_w, dn_a_log, dn_dt_bias,
                  dn_norm_g, w_branch_a, w_branch_b, w_out):
    n_ba = 2 * N_DIR * DN_HEADS
    w = w_in[l]
    c_ba = 3 * NA_WIDTH + NA_WIDTH + 3 * DN_WIDTH + DN_WIDTH
    w_re = jnp.concatenate(
        [w[:, :c_ba], w[:, c_ba + n_ba:], w[:, c_ba:c_ba + n_ba],
         jnp.zeros((D_MODEL, LANES - n_ba), w.dtype)], axis=1).astype(BF16)
    pad = lambda vec, lo: jnp.zeros((1, LANES), F32).at[0, lo:lo + vec.size].set(vec.reshape(-1).astype(F32))
    head_id = jnp.arange(NA_WIDTH // 2) // NA_HEAD_DIM
    tok_id = jnp.arange(DN_BLOCK)
    same_chunk = (tok_id[:, None] // CHUNK) == (tok_id[None, :] // CHUNK)
    return dict(
        norm_g=norm_g[l].reshape(1, D_MODEL),
        w_re=w_re,
        gsum=(head_id[:, None] == head_id[None, :]).astype(BF16),
        qg=(jnp.tile(attn_q_norm_g[l], NA_HEADS) * NA_HEAD_DIM ** -0.5).reshape(1, NA_WIDTH),
        kg=jnp.tile(attn_k_norm_g[l], NA_HEADS).reshape(1, NA_WIDTH),
        alog=pad(dn_a_log[l], N_DIR * DN_HEADS),
        dt=pad(dn_dt_bias[l], N_DIR * DN_HEADS),
        bias=_attn_bias_table(attn_rpb[l]),
        conv_w8=jnp.concatenate([dn_conv_w[l].astype(F32), jnp.zeros((8 - CONV_K, 3 * DN_WIDTH), F32)], axis=0),
        tri_l=(same_chunk & (tok_id[:, None] >= tok_id[None, :])).astype(BF16),
        tri_u=(same_chunk & (tok_id[:, None] <= tok_id[None, :])).astype(BF16),
        dn_g=dn_norm_g[l].reshape(1, DN_HEAD_DIM),
        wa=w_branch_a[l].astype(BF16),
        wb=w_branch_b[l].astype(BF16),
        wo=w_out[l].astype(BF16),
    )


def _layer(x, p):
    b, t, _ = x.shape
    assert t % TM_PROJ == 0 and t % DN_BLOCK == 0
    x2 = x.reshape(b * t, D_MODEL)
    q, k, v, sza, qkvc, szb, gates, bg = _in_proj(x2, t, p["norm_g"], p["w_re"], p["gsum"], p["qg"], p["kg"],
                                                  p["alog"], p["dt"], p["conv_w8"])
    r3 = lambda a: a.reshape(b, t, a.shape[-1])
    oa = _attention(r3(q), r3(k), r3(v), r3(sza), p["bias"])
    of, ob = _deltanet(r3(qkvc), r3(bg), p["tri_l"], p["tri_u"])
    y = _out_stage(x2, oa.reshape(b * t, NA_WIDTH), of.reshape(b * t, DN_WIDTH), ob.reshape(b * t, DN_WIDTH),
                   szb, gates, p["dn_g"], p["wa"], p["wb"], p["wo"])
    return y.reshape(b, t, D_MODEL)


def kernel(x_prompt, x_sample, norm_g, w_in, attn_q_norm_g, attn_k_norm_g, attn_rpb, dn_conv_w, dn_a_log,
           dn_dt_bias, dn_norm_g, w_branch_a, w_branch_b, w_out):
    depth = w_in.shape[0]
    params = [_layer_params(l, norm_g, w_in, attn_q_norm_g, attn_k_norm_g, attn_rpb, dn_conv_w, dn_a_log,
                            dn_dt_bias, dn_norm_g, w_branch_a, w_branch_b, w_out) for l in range(depth)]
    outs = []
    for x in (x_prompt, x_sample):
        for p in params:
            x = _layer(x, p)
        outs.append(x)
    return tuple(outs)
```

```python
import functools

import jax
import jax.numpy as jnp
from jax import lax
from jax.experimental import pallas as pl
from jax.experimental.pallas import tpu as pltpu

D_MODEL = 1024
GRID_W = 64
NA_HEADS = 8
NA_HEAD_DIM = 64
NA_WIDTH = NA_HEADS * NA_HEAD_DIM
WIN_R = 8
WIN_C = 16
DN_HEADS = 4
DN_HEAD_DIM = 128
DN_WIDTH = DN_HEADS * DN_HEAD_DIM
CONV_K = 5
CHUNK = 64
N_DIR = 2
EPS = 1e-6

LANES = 128
HALO = 16
MASK_VALUE = -1e30

C_QKV_A = 0
C_Z_A = 3 * NA_WIDTH
C_QKV_B = C_Z_A + NA_WIDTH
C_Z_B = C_QKV_B + 3 * DN_WIDTH
C_GATE = C_Z_B + DN_WIDTH
C_BG = C_GATE + 2 * D_MODEL
C_END = C_BG + LANES

TM_PROJ = 512
TM_OUT = 1024
ATTN_ROWS = 8
DN_BLOCK = 256
DN_BATCH = 2
VMEM_LIMIT = 48 * 1024 * 1024

F32 = jnp.float32
BF16 = jnp.bfloat16


def _const_spec(shape):
    return pl.BlockSpec(shape, lambda *_: (0,) * len(shape), pipeline_mode=pl.Buffered(1))


def _sigmoid(x):
    return 0.5 * jnp.tanh(0.5 * x) + 0.5


def _silu(x):
    return x * _sigmoid(x)


def _dot(a, b):
    return jnp.dot(a, b, preferred_element_type=F32)


def _dot_nt(a, b):
    return lax.dot_general(a, b, (((1,), (1,)), ((), ())), preferred_element_type=F32)


def _dot_tn(a, b):
    return lax.dot_general(a, b, (((0,), (0,)), ((), ())), preferred_element_type=F32)


def _in_proj_kernel(x_ref, xp_ref, xn_ref, g_ref, w_ref, gs_ref, qg_ref, kg_ref, alog_ref, dt_ref, cw_ref,
                    q_ref, k_ref, v_ref, sza_ref, qkvc_ref, szb_ref, gate_ref, bg_ref, pb_ref, *, tiles_per_seq):
    i = pl.program_id(0)
    tm = TM_PROJ
    x = jnp.concatenate([xp_ref[...], x_ref[...], xn_ref[...]], axis=0)
    ms = jnp.mean(x * x, axis=-1, keepdims=True)
    h_ext = (x * lax.rsqrt(ms + EPS) * g_ref[...]).astype(BF16)
    h = h_ext[HALO:HALO + tm]
    gs = gs_ref[...]

    def proj(lo, hi):
        return _dot(h, w_ref[:, lo:hi])

    def head_norm(y, gain):
        y2 = (y * y).astype(BF16)
        half = NA_WIDTH // 2
        ss = jnp.concatenate([_dot(y2[:, :half], gs), _dot(y2[:, half:], gs)], axis=1)
        return y * lax.rsqrt(ss * (1.0 / NA_HEAD_DIM) + EPS) * gain

    row = lax.broadcasted_iota(jnp.int32, (tm + 2 * HALO, 1), 0)
    seq_tile = i % tiles_per_seq
    valid = ((row >= HALO) | (seq_tile > 0)) & ((row < HALO + tm) | (seq_tile < tiles_per_seq - 1))
    def ordering_zero(y):
        bits = pltpu.bitcast(y[0:8, 0:LANES], jnp.uint32)
        bits = lax.shift_right_logical(lax.shift_right_logical(bits, jnp.uint32(16)), jnp.uint32(16))
        return pltpu.bitcast(bits, F32)[0:1, :]

    def conv_slab(slab, zero_row):
        cols = slice(slab * DN_HEAD_DIM, (slab + 1) * DN_HEAD_DIM)
        acc = zero_row
        for j in range(CONV_K):
            start = HALO - CONV_K // 2 + j
            acc = acc + pb_ref[start:start + tm, cols] * cw_ref[j:j + 1, cols]
        y = _silu(acc)
        if slab < 2 * DN_HEADS:
            y = y * lax.rsqrt(jnp.sum(y * y, axis=-1, keepdims=True) + EPS)
        if slab < DN_HEADS:
            y = y * (DN_HEAD_DIM ** -0.5)
        qkvc_ref[:, cols] = y.astype(BF16)

    slabs = iter(range(3 * DN_HEADS))

    def conv_some(count, after):
        zero_row = ordering_zero(after)
        for _ in range(count):
            conv_slab(next(slabs), zero_row)

    for part in range(3):
        cols = slice(part * DN_WIDTH, (part + 1) * DN_WIDTH)
        lo = C_QKV_B + part * DN_WIDTH
        pr = _dot(h_ext, w_ref[:, lo:lo + DN_WIDTH])
        pb_ref[:, cols] = jnp.where(valid, pr, 0.0)
        if part > 0:
            conv_some(1, pr)
    pr = proj(C_QKV_A, C_QKV_A + NA_WIDTH)
    q_ref[...] = head_norm(pr, qg_ref[...]).astype(BF16)
    conv_some(1, pr)
    pr = proj(C_QKV_A + NA_WIDTH, C_QKV_A + 2 * NA_WIDTH)
    k_ref[...] = head_norm(pr, kg_ref[...]).astype(BF16)
    conv_some(1, pr)
    pr = proj(C_QKV_A + 2 * NA_WIDTH, C_Z_A)
    v_ref[...] = pr.astype(BF16)
    conv_some(1, pr)
    pr = proj(C_Z_A, C_QKV_B)
    sza_ref[...] = _silu(pr).astype(BF16)
    conv_some(1, pr)
    pr = proj(C_Z_B, C_GATE)
    szb_ref[...] = _silu(pr).astype(BF16)
    conv_some(1, pr)
    for part in range(4):
        lo = C_GATE + part * DN_WIDTH
        pr = proj(lo, lo + DN_WIDTH)
        gate_ref[:, part * DN_WIDTH:(part + 1) * DN_WIDTH] = _sigmoid(pr).astype(BF16)
        conv_some(1, pr)
    ba = proj(C_BG, C_END)
    lane = lax.broadcasted_iota(jnp.int32, ba.shape, 1)
    z = ba + dt_ref[...]
    sp = jnp.maximum(z, 0.0) + jnp.log1p(jnp.exp(-jnp.abs(z)))
    g = -jnp.exp(alog_ref[...]) * sp
    is_g = (lane >= N_DIR * DN_HEADS) & (lane < 2 * N_DIR * DN_HEADS)
    bg_ref[...] = jnp.where(lane < N_DIR * DN_HEADS, _sigmoid(ba), jnp.where(is_g, g, 0.0))
    conv_some(1, ba)


def _in_proj(x2, seq_len, norm_g, w_re, gsum, qg, kg, alog_vec, dt_vec, conv_w8):
    n = x2.shape[0]
    tm = TM_PROJ
    hpt = tm // HALO
    n_halo = n // HALO
    tok = lambda width: pl.BlockSpec((tm, width), lambda i: (i, 0))
    out_shapes = (
        jax.ShapeDtypeStruct((n, NA_WIDTH), BF16),
        jax.ShapeDtypeStruct((n, NA_WIDTH), BF16),
        jax.ShapeDtypeStruct((n, NA_WIDTH), BF16),
        jax.ShapeDtypeStruct((n, NA_WIDTH), BF16),
        jax.ShapeDtypeStruct((n, 3 * DN_WIDTH), BF16),
        jax.ShapeDtypeStruct((n, DN_WIDTH), BF16),
        jax.ShapeDtypeStruct((n, 2 * D_MODEL), BF16),
        jax.ShapeDtypeStruct((n, LANES), F32),
    )
    return pl.pallas_call(
        functools.partial(_in_proj_kernel, tiles_per_seq=seq_len // tm),
        grid=(n // tm,),
        in_specs=[tok(D_MODEL),
                  pl.BlockSpec((HALO, D_MODEL), lambda i: (jnp.maximum(i * hpt - 1, 0), 0)),
                  pl.BlockSpec((HALO, D_MODEL), lambda i: (jnp.minimum((i + 1) * hpt, n_halo - 1), 0)),
                  _const_spec((1, D_MODEL)), _const_spec((D_MODEL, C_END)),
                  _const_spec((NA_WIDTH // 2, NA_WIDTH // 2)), _const_spec((1, NA_WIDTH)),
                  _const_spec((1, NA_WIDTH)),
                  _const_spec((1, LANES)), _const_spec((1, LANES)), _const_spec((8, 3 * DN_WIDTH))],
        out_specs=[tok(NA_WIDTH), tok(NA_WIDTH), tok(NA_WIDTH), tok(NA_WIDTH), tok(3 * DN_WIDTH),
                   tok(DN_WIDTH), tok(2 * D_MODEL), tok(LANES)],
        out_shape=out_shapes,
        scratch_shapes=[pltpu.VMEM((tm + 2 * HALO, 3 * DN_WIDTH), F32)],
        compiler_params=pltpu.CompilerParams(dimension_semantics=("arbitrary",),
                                             vmem_limit_bytes=VMEM_LIMIT),
        name="in_proj",
    )(x2, x2, x2, norm_g, w_re, gsum, qg, kg, alog_vec, dt_vec, conv_w8)


def _attn_kernel(q_ref, k_ref, v_ref, sza_ref, bias_ref, o_ref, *, rows):
    blk = pl.program_id(1)
    lane = lax.broadcasted_iota(jnp.int32, (GRID_W, LANES), 1)
    lo_half = lane < NA_HEAD_DIM
    n_keys = WIN_R * GRID_W
    n_pairs = NA_HEADS // 2
    ones = jnp.ones((n_keys, LANES), BF16)

    def row_body(rl, carry):
        r = blk * ATTN_ROWS + rl
        rs = jnp.clip(r - WIN_R // 2, 0, rows - WIN_R)
        var = r - rs
        kstart = pl.multiple_of(rs * GRID_W, GRID_W)
        qstart = pl.multiple_of(rl * GRID_W, GRID_W)
        st = []
        for p in range(n_pairs):
            cols = slice(p * LANES, (p + 1) * LANES)
            q2 = q_ref[0, pl.ds(qstart, GRID_W), cols]
            zero = jnp.zeros_like(q2)
            qm = jnp.concatenate([jnp.where(lo_half, q2, zero), jnp.where(lo_half, zero, q2)], axis=0)
            k2 = k_ref[0, pl.ds(kstart, n_keys), cols]
            sd = _dot_nt(k2, qm)
            st.append(jnp.concatenate(
                [sd[kr * GRID_W:(kr + 1) * GRID_W] + bias_ref[p, kr + WIN_R - 1 - var] for kr in range(WIN_R)],
                axis=0))
        mx = [jnp.max(s, axis=0, keepdims=True) for s in st]
        pe = [jnp.exp(s - m).astype(BF16) for s, m in zip(st, mx)]
        for p in range(n_pairs):
            cols = slice(p * LANES, (p + 1) * LANES)
            v2 = v_ref[0, pl.ds(kstart, n_keys), cols]
            ox = _dot_tn(pe[p], jnp.concatenate([v2, ones], axis=1))
            on = ox[:, :LANES] / ox[:, LANES:]
            o2 = jnp.where(lo_half, on[:GRID_W], on[GRID_W:])
            sz = sza_ref[0, pl.ds(qstart, GRID_W), cols].astype(F32)
            o_ref[0, pl.ds(qstart, GRID_W), cols] = (o2 * sz).astype(BF16)
        return carry

    lax.fori_loop(0, ATTN_ROWS, row_body, 0, unroll=8)


def _attention(q, k, v, sza, bias):
    b, t, _ = q.shape
    rows = t // GRID_W
    assert rows >= WIN_R and rows % ATTN_ROWS == 0
    tq = ATTN_ROWS * GRID_W
    blk = pl.BlockSpec((1, tq, NA_WIDTH), lambda bi, i: (bi, i, 0))
    full = pl.BlockSpec((1, t, NA_WIDTH), lambda bi, i: (bi, 0, 0), pipeline_mode=pl.Buffered(1))
    return pl.pallas_call(
        functools.partial(_attn_kernel, rows=rows),
        grid=(b, rows // ATTN_ROWS),
        in_specs=[blk, full, full, blk, _const_spec(bias.shape)],
        out_specs=blk,
        out_shape=jax.ShapeDtypeStruct((b, t, NA_WIDTH), BF16),
        compiler_params=pltpu.CompilerParams(dimension_semantics=("arbitrary", "arbitrary"),
                                             vmem_limit_bytes=VMEM_LIMIT),
        name="attn",
    )(q, k, v, sza, bias)


def _attn_bias_table(rpb):
    n_dr = 2 * WIN_R - 1
    n_dc = 2 * WIN_C - 1
    qc = jnp.arange(GRID_W)
    kc = jnp.arange(GRID_W)
    col_start = jnp.clip(qc - WIN_C // 2, 0, GRID_W - WIN_C)
    ok = (kc[None, :] >= col_start[:, None]) & (kc[None, :] < col_start[:, None] + WIN_C)
    period = 2 * GRID_W
    lead = GRID_W - WIN_C
    ext = jnp.pad(rpb.astype(F32), ((0, 0), (0, 0), (lead, period - lead - n_dc)))
    flat = jnp.tile(ext, (1, 1, GRID_W))[:, :, :GRID_W * (period - 1)]
    band = flat.reshape(NA_HEADS, n_dr, GRID_W, period - 1)[:, :, :, GRID_W - 1:]
    band = jnp.where(ok[None, None], band, MASK_VALUE)
    band = band.reshape(NA_HEADS // 2, 2, n_dr, GRID_W, GRID_W)
    band = jnp.transpose(band, (0, 2, 4, 1, 3))
    return band.reshape(NA_HEADS // 2, n_dr, GRID_W, 2 * GRID_W)


def _split3_dot(tri, x):
    x1 = x.astype(BF16)
    r1 = x - x1.astype(F32)
    x2 = r1.astype(BF16)
    x3 = (r1 - x2.astype(F32)).astype(BF16)
    return _dot(tri, x1) + _dot(tri, x2) + _dot(tri, x3)


def _dn_kernel(xf_ref, xb_ref, bgf_ref, bgb_ref, tl_ref, tu_ref, of_ref, ob_ref, s_ref):
    i = pl.program_id(1)
    tb = DN_BLOCK
    nc = tb // CHUNK

    @pl.when(i == 0)
    def _():
        s_ref[...] = jnp.zeros_like(s_ref)

    half = tb // 2
    ri = lax.broadcasted_iota(jnp.int32, (tb, half), 0) % half
    nn = lax.broadcasted_iota(jnp.int32, (tb, half), 1)
    same = (ri // CHUNK) == (nn // CHUNK)
    eye = (ri == nn).astype(F32)
    lane = lax.broadcasted_iota(jnp.int32, (tb, LANES), 1)
    is_g = (lane >= N_DIR * DN_HEADS) & (lane < 2 * N_DIR * DN_HEADS)

    def block_diag(x):
        z = jnp.zeros((half, half), x.dtype)
        return jnp.concatenate([jnp.concatenate([x[:half], z], axis=1),
                                jnp.concatenate([z, x[half:]], axis=1)], axis=0)

    def gram(a, b):
        return jnp.concatenate([_dot_nt(a[:half], b[:half]), _dot_nt(a[half:], b[half:])], axis=0)

    chains = []
    for bb, d in [(bb, d) for bb in range(DN_BATCH) for d in range(N_DIR)]:
        x_ref = xf_ref if d == 0 else xb_ref
        bg_ref = bgf_ref if d == 0 else bgb_ref
        incl = same & ((ri >= nn) if d == 0 else (ri <= nn))
        strict = same & ((ri > nn) if d == 0 else (ri < nn))

        bg = bg_ref[bb]
        gvals = jnp.where(is_g, bg, 0.0)
        gc = _split3_dot(tl_ref[...] if d == 0 else tu_ref[...], gvals)
        gct = gc.T
        eg = jnp.exp(gc)

        for h in range(DN_HEADS):
            bi = d * DN_HEADS + h
            ci = N_DIR * DN_HEADS + bi
            q_bf = x_ref[bb, :, h * DN_HEAD_DIM:(h + 1) * DN_HEAD_DIM]
            k_bf = x_ref[bb, :, DN_WIDTH + h * DN_HEAD_DIM:DN_WIDTH + (h + 1) * DN_HEAD_DIM]
            v_bf = x_ref[bb, :, 2 * DN_WIDTH + h * DN_HEAD_DIM:2 * DN_WIDTH + (h + 1) * DN_HEAD_DIM]
            kh = k_bf.astype(F32)
            beta = bg[:, bi:bi + 1]
            gcol_all = gc[:, ci:ci + 1]
            egcol = eg[:, ci:ci + 1]
            rhs = jnp.concatenate([kh * (beta * egcol), v_bf.astype(F32) * beta], axis=1).astype(BF16)
            kk = gram(k_bf, k_bf)
            qk = gram(q_bf, k_bf)
            diff = jnp.concatenate([gcol_all[:half] - gct[ci:ci + 1, :half],
                                    gcol_all[half:] - gct[ci:ci + 1, half:]], axis=0)
            decay = jnp.where(incl, jnp.exp(jnp.where(incl, diff, 0.0)), 0.0)
            glast, kdec = [], []
            for c in range(nc):
                rc = slice(c * CHUNK, (c + 1) * CHUNK)
                row = (c + 1) * CHUNK - 1 if d == 0 else c * CHUNK
                gl = gcol_all[row:row + 1]
                glast.append(gl)
                kdec.append((kh[rc] * jnp.exp(gl - gcol_all[rc])).astype(BF16))
            chains.append(dict(
                d=d, h=h, bb=bb, si=bb * N_DIR * DN_HEADS + bi,
                m=jnp.where(strict, -(kk * beta * decay), 0.0),
                intra=(qk * decay).astype(BF16),
                rhs=rhs, qdec=q_bf.astype(F32) * egcol, glast=glast, kdec=kdec))

    xs = [ch["m"].astype(BF16) for ch in chains]
    ps = [eye + ch["m"] for ch in chains]
    xs = [_dot(block_diag(x), x).astype(BF16) for x in xs]
    for _ in range(4):
        outs = [_dot(block_diag(x), jnp.concatenate([x, p.astype(BF16)], axis=1)) for x, p in zip(xs, ps)]
        xs = [o[:, :half].astype(BF16) for o in outs]
        ps = [p + o[:, half:] for p, o in zip(ps, outs)]
    ps = [p + _dot(block_diag(x), p.astype(BF16)) for x, p in zip(xs, ps)]

    for ch, p in zip(chains, ps):
        wu = _dot(block_diag(p.astype(BF16)), ch["rhs"]).astype(BF16)
        iw = _dot(block_diag(ch["intra"]), wu)
        ch["qp"] = (ch["qdec"] - iw[:, :DN_HEAD_DIM]).astype(BF16)
        ch["oi"] = iw[:, DN_HEAD_DIM:]
        ch["gh"] = [_dot_tn(ch["kdec"][c], wu[c * CHUNK:(c + 1) * CHUNK]) for c in range(nc)]
        ch["s"] = s_ref[ch["si"]]

    for step in range(nc):
        for ch in chains:
            c = step if ch["d"] == 0 else nc - 1 - step
            rc = slice(c * CHUNK, (c + 1) * CHUNK)
            o_ref = of_ref if ch["d"] == 0 else ob_ref
            s = ch["s"]
            sb = s.astype(BF16)
            o = _dot(ch["qp"][rc], sb) + ch["oi"][rc]
            hs = slice(ch["h"] * DN_HEAD_DIM, (ch["h"] + 1) * DN_HEAD_DIM)
            o_ref[ch["bb"], rc, hs] = o.astype(o_ref.dtype)
            gh = ch["gh"][c]
            ch["s"] = (s * jnp.exp(ch["glast"][c]) + gh[:, DN_HEAD_DIM:]
                       - _dot(gh[:, :DN_HEAD_DIM].astype(BF16), sb))

    for ch in chains:
        s_ref[ch["si"]] = ch["s"]


def _deltanet(qkvc, bg, tri_l, tri_u):
    b, t, _ = qkvc.shape
    tb = DN_BLOCK
    nblk = t // tb
    c3 = 3 * DN_WIDTH
    nb = DN_BATCH
    assert b % nb == 0
    fwd = lambda bi, i: (bi, i, 0)
    bwd = lambda bi, i: (bi, nblk - 1 - i, 0)
    return pl.pallas_call(
        _dn_kernel,
        grid=(b // nb, nblk),
        in_specs=[pl.BlockSpec((nb, tb, c3), fwd), pl.BlockSpec((nb, tb, c3), bwd),
                  pl.BlockSpec((nb, tb, LANES), fwd), pl.BlockSpec((nb, tb, LANES), bwd),
                  _const_spec((tb, tb)), _const_spec((tb, tb))],
        out_specs=[pl.BlockSpec((nb, tb, DN_WIDTH), fwd), pl.BlockSpec((nb, tb, DN_WIDTH), bwd)],
        out_shape=(jax.ShapeDtypeStruct((b, t, DN_WIDTH), BF16),
                   jax.ShapeDtypeStruct((b, t, DN_WIDTH), BF16)),
        scratch_shapes=[pltpu.VMEM((nb * N_DIR * DN_HEADS, DN_HEAD_DIM, DN_HEAD_DIM), F32)],
        compiler_params=pltpu.CompilerParams(dimension_semantics=("arbitrary", "arbitrary"),
                                             vmem_limit_bytes=VMEM_LIMIT),
        name="deltanet",
    )(qkvc, qkvc, bg, bg, tri_l, tri_u)


def _out_kernel(x_ref, oa_ref, of_ref, ob_ref, szb_ref, gate_ref, ng_ref, wa_ref, wb_ref, wo_ref, y_ref):
    o = of_ref[...].astype(F32) + ob_ref[...].astype(F32)
    parts = []
    for h in range(DN_HEADS):
        oh = o[:, h * DN_HEAD_DIM:(h + 1) * DN_HEAD_DIM]
        ms = jnp.mean(oh * oh, axis=-1, keepdims=True)
        parts.append(oh * lax.rsqrt(ms + EPS) * ng_ref[...])
    on = jnp.concatenate(parts, axis=1)
    o_b = (on * szb_ref[...].astype(F32)).astype(BF16)
    ya = _dot(oa_ref[...], wa_ref[...])
    yb = _dot(o_b, wb_ref[...])
    merged = (gate_ref[:, :D_MODEL].astype(F32) * ya + gate_ref[:, D_MODEL:].astype(F32) * yb).astype(BF16)
    y_ref[...] = x_ref[...] + _dot(merged, wo_ref[...])


def _out_stage(x2, oa, of, ob, szb, gates, ng, wa, wb, wo):
    n = x2.shape[0]
    tm = TM_OUT
    tok = lambda width: pl.BlockSpec((tm, width), lambda i: (i, 0))
    return pl.pallas_call(
        _out_kernel,
        grid=(n // tm,),
        in_specs=[tok(D_MODEL), tok(NA_WIDTH), tok(DN_WIDTH), tok(DN_WIDTH), tok(DN_WIDTH), tok(2 * D_MODEL),
                  _const_spec((1, DN_HEAD_DIM)), _const_spec((NA_WIDTH, D_MODEL)),
                  _const_spec((DN_WIDTH, D_MODEL)), _const_spec((D_MODEL, D_MODEL))],
        out_specs=tok(D_MODEL),
        out_shape=jax.ShapeDtypeStruct((n, D_MODEL), F32),
        compiler_params=pltpu.CompilerParams(dimension_semantics=("arbitrary",),
                                             vmem_limit_bytes=VMEM_LIMIT),
        name="out_stage",
    )(x2, oa, of, ob, szb, gates, ng, wa, wb, wo)


def _layer_params(l, norm_g, w_in, attn_q_norm_g, attn_k_norm_g, attn_rpb, dn_conv_w, dn_a_log, dn_dt_bias,
                  dn_norm_g, w_branch_a, w_branch_b, w_out):
    n_ba = 2 * N_DIR * DN_HEADS
    w = w_in[l]
    c_ba = 3 * NA_WIDTH + NA_WIDTH + 3 * DN_WIDTH + DN_WIDTH
    w_re = jnp.concatenate(
        [w[:, :c_ba], w[:, c_ba + n_ba:], w[:, c_ba:c_ba + n_ba],
         jnp.zeros((D_MODEL, LANES - n_ba), w.dtype)], axis=1).astype(BF16)
    pad = lambda vec, lo: jnp.zeros((1, LANES), F32).at[0, lo:lo + vec.size].set(vec.reshape(-1).astype(F32))
    head_id = jnp.arange(NA_WIDTH // 2) // NA_HEAD_DIM
    tok_id = jnp.arange(DN_BLOCK)
    same_chunk = (tok_id[:, None] // CHUNK) == (tok_id[None, :] // CHUNK)
    return dict(
        norm_g=norm_g[l].reshape(1, D_MODEL),
        w_re=w_re,
        gsum=(head_id[:, None] == head_id[None, :]).astype(BF16),
        qg=(jnp.tile(attn_q_norm_g[l], NA_HEADS) * NA_HEAD_DIM ** -0.5).reshape(1, NA_WIDTH),
        kg=jnp.tile(attn_k_norm_g[l], NA_HEADS).reshape(1, NA_WIDTH),
        alog=pad(dn_a_log[l], N_DIR * DN_HEADS),
        dt=pad(dn_dt_bias[l], N_DIR * DN_HEADS),
        bias=_attn_bias_table(attn_rpb[l]),
        conv_w8=jnp.concatenate([dn_conv_w[l].astype(F32), jnp.zeros((8 - CONV_K, 3 * DN_WIDTH), F32)], axis=0),
        tri_l=(same_chunk & (tok_id[:, None] >= tok_id[None, :])).astype(BF16),
        tri_u=(same_chunk & (tok_id[:, None] <= tok_id[None, :])).astype(BF16),
        dn_g=dn_norm_g[l].reshape(1, DN_HEAD_DIM),
        wa=w_branch_a[l].astype(BF16),
        wb=w_branch_b[l].astype(BF16),
        wo=w_out[l].astype(BF16),
    )


def _layer(x, p):
    b, t, _ = x.shape
    assert t % TM_PROJ == 0 and t % DN_BLOCK == 0 and (b * t) % TM_OUT == 0
    x2 = x.reshape(b * t, D_MODEL)
    q, k, v, sza, qkvc, szb, gates, bg = _in_proj(x2, t, p["norm_g"], p["w_re"], p["gsum"], p["qg"], p["kg"],
                                                  p["alog"], p["dt"], p["conv_w8"])
    r3 = lambda a: a.reshape(b, t, a.shape[-1])
    oa = _attention(r3(q), r3(k), r3(v), r3(sza), p["bias"])
    of, ob = _deltanet(r3(qkvc), r3(bg), p["tri_l"], p["tri_u"])
    y = _out_stage(x2, oa.reshape(b * t, NA_WIDTH), of.reshape(b * t, DN_WIDTH), ob.reshape(b * t, DN_WIDTH),
                   szb, gates, p["dn_g"], p["wa"], p["wb"], p["wo"])
    return y.reshape(b, t, D_MODEL)


def kernel(x_prompt, x_sample, norm_g, w_in, attn_q_norm_g, attn_k_norm_g, attn_rpb, dn_conv_w, dn_a_log,
           dn_dt_bias, dn_norm_g, w_branch_a, w_branch_b, w_out):
    depth = w_in.shape[0]
    params = [_layer_params(l, norm_g, w_in, attn_q_norm_g, attn_k_norm_g, attn_rpb, dn_conv_w, dn_a_log,
                            dn_dt_bias, dn_norm_g, w_branch_a, w_branch_b, w_out) for l in range(depth)]
    outs = []
    for x in (x_prompt, x_sample):
        for p in params:
            x = _layer(x, p)
        outs.append(x)
    return tuple(outs)
```

```python
import functools

import jax
import jax.numpy as jnp
from jax import lax
from jax.experimental import pallas as pl
from jax.experimental.pallas import tpu as pltpu

D_MODEL = 1024
GRID_W = 64
NA_HEADS = 8
NA_HEAD_DIM = 64
NA_WIDTH = NA_HEADS * NA_HEAD_DIM
WIN_R = 8
WIN_C = 16
DN_HEADS = 4
DN_HEAD_DIM = 128
DN_WIDTH = DN_HEADS * DN_HEAD_DIM
CONV_K = 5
CHUNK = 64
N_DIR = 2
EPS = 1e-6

LANES = 128
HALO = 16
MASK_VALUE = -1e30

C_QKV_A = 0
C_Z_A = 3 * NA_WIDTH
C_QKV_B = C_Z_A + NA_WIDTH
C_Z_B = C_QKV_B + 3 * DN_WIDTH
C_GATE = C_Z_B + DN_WIDTH
C_BG = C_GATE + 2 * D_MODEL
C_END = C_BG + LANES

TM_PROJ = 512
TM_OUT = 1024
ATTN_ROWS = 8
DN_BLOCK = 256
DN_BATCH = 2
VMEM_LIMIT = 48 * 1024 * 1024

F32 = jnp.float32
BF16 = jnp.bfloat16


def _const_spec(shape):
    return pl.BlockSpec(shape, lambda *_: (0,) * len(shape), pipeline_mode=pl.Buffered(1))


def _sigmoid(x):
    return 0.5 * jnp.tanh(0.5 * x) + 0.5


def _silu(x):
    h = 0.5 * x
    return h + h * jnp.tanh(h)


def _dot(a, b):
    return jnp.dot(a, b, preferred_element_type=F32)


def _dot_nt(a, b):
    return lax.dot_general(a, b, (((1,), (1,)), ((), ())), preferred_element_type=F32)


def _dot_tn(a, b):
    return lax.dot_general(a, b, (((0,), (0,)), ((), ())), preferred_element_type=F32)


def _in_proj_kernel(x_ref, xp_ref, xn_ref, g_ref, w_ref, gs_ref, qg_ref, kg_ref, alog_ref, dt_ref, cw_ref,
                    q_ref, k_ref, v_ref, sza_ref, qkvc_ref, szb_ref, gate_ref, bg_ref, pb_ref, *, tiles_per_seq):
    i = pl.program_id(0)
    tm = TM_PROJ
    x = jnp.concatenate([xp_ref[...], x_ref[...], xn_ref[...]], axis=0)
    ms = jnp.mean(x * x, axis=-1, keepdims=True)
    h_ext = (x * lax.rsqrt(ms + EPS) * g_ref[...]).astype(BF16)
    h = h_ext[HALO:HALO + tm]
    gs = gs_ref[...]

    def proj(lo, hi):
        return _dot(h, w_ref[:, lo:hi])

    def head_norm(y, gain):
        y2 = (y * y).astype(BF16)
        half = NA_WIDTH // 2
        ss = jnp.concatenate([_dot(y2[:, :half], gs), _dot(y2[:, half:], gs)], axis=1)
        return y * lax.rsqrt(ss * (1.0 / NA_HEAD_DIM) + EPS) * gain

    row = lax.broadcasted_iota(jnp.int32, (tm + 2 * HALO, 1), 0)
    seq_tile = i % tiles_per_seq
    valid = ((row >= HALO) | (seq_tile > 0)) & ((row < HALO + tm) | (seq_tile < tiles_per_seq - 1))
    def ordering_zero(y):
        bits = pltpu.bitcast(y[0:8, 0:LANES], jnp.uint32)
        bits = lax.shift_right_logical(lax.shift_right_logical(bits, jnp.uint32(16)), jnp.uint32(16))
        return pltpu.bitcast(bits, F32)[0:1, :]

    def conv_slab(slab, zero_row):
        cols = slice(slab * DN_HEAD_DIM, (slab + 1) * DN_HEAD_DIM)
        acc = zero_row
        for j in range(CONV_K):
            start = HALO - CONV_K // 2 + j
            acc = acc + pb_ref[start:start + tm, cols] * cw_ref[j:j + 1, cols]
        y = _silu(acc)
        if slab < 2 * DN_HEADS:
            y = y * lax.rsqrt(jnp.sum(y * y, axis=-1, keepdims=True) + EPS)
        if slab < DN_HEADS:
            y = y * (DN_HEAD_DIM ** -0.5)
        qkvc_ref[:, cols] = y.astype(BF16)

    slabs = iter(range(3 * DN_HEADS))

    def conv_some(count, after):
        zero_row = ordering_zero(after)
        for _ in range(count):
            conv_slab(next(slabs), zero_row)

    for part in range(3):
        cols = slice(part * DN_WIDTH, (part + 1) * DN_WIDTH)
        lo = C_QKV_B + part * DN_WIDTH
        pr = _dot(h_ext, w_ref[:, lo:lo + DN_WIDTH])
        pb_ref[:, cols] = jnp.where(valid, pr, 0.0)
        if part > 0:
            conv_some(1, pr)
    pr = proj(C_QKV_A, C_QKV_A + NA_WIDTH)
    q_ref[...] = head_norm(pr, qg_ref[...]).astype(BF16)
    conv_some(1, pr)
    pr = proj(C_QKV_A + NA_WIDTH, C_QKV_A + 2 * NA_WIDTH)
    k_ref[...] = head_norm(pr, kg_ref[...]).astype(BF16)
    conv_some(1, pr)
    pr = proj(C_QKV_A + 2 * NA_WIDTH, C_Z_A)
    v_ref[...] = pr.astype(BF16)
    conv_some(1, pr)
    pr = proj(C_Z_A, C_QKV_B)
    sza_ref[...] = _silu(pr).astype(BF16)
    conv_some(1, pr)
    pr = proj(C_Z_B, C_GATE)
    szb_ref[...] = _silu(pr).astype(BF16)
    conv_some(1, pr)
    for part in range(4):
        lo = C_GATE + part * DN_WIDTH
        pr = proj(lo, lo + DN_WIDTH)
        gate_ref[:, part * DN_WIDTH:(part + 1) * DN_WIDTH] = _sigmoid(pr).astype(BF16)
        conv_some(1, pr)
    ba = proj(C_BG, C_END)
    lane = lax.broadcasted_iota(jnp.int32, ba.shape, 1)
    z = ba + dt_ref[...]
    sp = jnp.maximum(z, 0.0) + jnp.log1p(jnp.exp(-jnp.abs(z)))
    g = -jnp.exp(alog_ref[...]) * sp
    is_g = (lane >= N_DIR * DN_HEADS) & (lane < 2 * N_DIR * DN_HEADS)
    bg_ref[...] = jnp.where(lane < N_DIR * DN_HEADS, _sigmoid(ba), jnp.where(is_g, g, 0.0))
    conv_some(1, ba)


def _in_proj(x2, seq_len, norm_g, w_re, gsum, qg, kg, alog_vec, dt_vec, conv_w8):
    n = x2.shape[0]
    tm = TM_PROJ
    hpt = tm // HALO
    n_halo = n // HALO
    tok = lambda width: pl.BlockSpec((tm, width), lambda i: (i, 0))
    out_shapes = (
        jax.ShapeDtypeStruct((n, NA_WIDTH), BF16),
        jax.ShapeDtypeStruct((n, NA_WIDTH), BF16),
        jax.ShapeDtypeStruct((n, NA_WIDTH), BF16),
        jax.ShapeDtypeStruct((n, NA_WIDTH), BF16),
        jax.ShapeDtypeStruct((n, 3 * DN_WIDTH), BF16),
        jax.ShapeDtypeStruct((n, DN_WIDTH), BF16),
        jax.ShapeDtypeStruct((n, 2 * D_MODEL), BF16),
        jax.ShapeDtypeStruct((n, LANES), F32),
    )
    return pl.pallas_call(
        functools.partial(_in_proj_kernel, tiles_per_seq=seq_len // tm),
        grid=(n // tm,),
        in_specs=[tok(D_MODEL),
                  pl.BlockSpec((HALO, D_MODEL), lambda i: (jnp.maximum(i * hpt - 1, 0), 0)),
                  pl.BlockSpec((HALO, D_MODEL), lambda i: (jnp.minimum((i + 1) * hpt, n_halo - 1), 0)),
                  _const_spec((1, D_MODEL)), _const_spec((D_MODEL, C_END)),
                  _const_spec((NA_WIDTH // 2, NA_WIDTH // 2)), _const_spec((1, NA_WIDTH)),
                  _const_spec((1, NA_WIDTH)),
                  _const_spec((1, LANES)), _const_spec((1, LANES)), _const_spec((8, 3 * DN_WIDTH))],
        out_specs=[tok(NA_WIDTH), tok(NA_WIDTH), tok(NA_WIDTH), tok(NA_WIDTH), tok(3 * DN_WIDTH),
                   tok(DN_WIDTH), tok(2 * D_MODEL), tok(LANES)],
        out_shape=out_shapes,
        scratch_shapes=[pltpu.VMEM((tm + 2 * HALO, 3 * DN_WIDTH), F32)],
        compiler_params=pltpu.CompilerParams(dimension_semantics=("arbitrary",),
                                             vmem_limit_bytes=VMEM_LIMIT),
        name="in_proj",
    )(x2, x2, x2, norm_g, w_re, gsum, qg, kg, alog_vec, dt_vec, conv_w8)


def _attn_kernel(q_ref, k_ref, v_ref, sza_ref, bias_ref, o_ref, *, rows):
    blk = pl.program_id(1)
    lane = lax.broadcasted_iota(jnp.int32, (GRID_W, LANES), 1)
    lo_half = lane < NA_HEAD_DIM
    n_keys = WIN_R * GRID_W
    n_pairs = NA_HEADS // 2
    ones = jnp.ones((n_keys, LANES), BF16)

    def row_body(rl, carry):
        r = blk * ATTN_ROWS + rl
        rs = jnp.clip(r - WIN_R // 2, 0, rows - WIN_R)
        var = r - rs
        kstart = pl.multiple_of(rs * GRID_W, GRID_W)
        qstart = pl.multiple_of(rl * GRID_W, GRID_W)
        st = []
        for p in range(n_pairs):
            cols = slice(p * LANES, (p + 1) * LANES)
            q2 = q_ref[0, pl.ds(qstart, GRID_W), cols]
            zero = jnp.zeros_like(q2)
            qm = jnp.concatenate([jnp.where(lo_half, q2, zero), jnp.where(lo_half, zero, q2)], axis=0)
            k2 = k_ref[0, pl.ds(kstart, n_keys), cols]
            sd = _dot_nt(qm, k2)
            st.append(jnp.concatenate(
                [sd[:, j * LANES:(j + 1) * LANES] + bias_ref[p, 2 * j + WIN_R - 1 - var] for j in range(WIN_R // 2)],
                axis=1))
        mx = [jnp.max(s, axis=-1, keepdims=True) for s in st]
        pe = [jnp.exp(s - m).astype(BF16) for s, m in zip(st, mx)]
        for p in range(n_pairs):
            cols = slice(p * LANES, (p + 1) * LANES)
            v2 = v_ref[0, pl.ds(kstart, n_keys), cols]
            ox = _dot(pe[p], jnp.concatenate([v2, ones], axis=1))
            on = ox[:, :LANES] / ox[:, LANES:]
            o2 = jnp.where(lo_half, on[:GRID_W], on[GRID_W:])
            sz = sza_ref[0, pl.ds(qstart, GRID_W), cols].astype(F32)
            o_ref[0, pl.ds(qstart, GRID_W), cols] = (o2 * sz).astype(BF16)
        return carry

    lax.fori_loop(0, ATTN_ROWS, row_body, 0, unroll=8)


def _attention(q, k, v, sza, bias):
    b, t, _ = q.shape
    rows = t // GRID_W
    assert rows >= WIN_R and rows % ATTN_ROWS == 0
    tq = ATTN_ROWS * GRID_W
    blk = pl.BlockSpec((1, tq, NA_WIDTH), lambda bi, i: (bi, i, 0))
    full = pl.BlockSpec((1, t, NA_WIDTH), lambda bi, i: (bi, 0, 0), pipeline_mode=pl.Buffered(1))
    return pl.pallas_call(
        functools.partial(_attn_kernel, rows=rows),
        grid=(b, rows // ATTN_ROWS),
        in_specs=[blk, full, full, blk, _const_spec(bias.shape)],
        out_specs=blk,
        out_shape=jax.ShapeDtypeStruct((b, t, NA_WIDTH), BF16),
        compiler_params=pltpu.CompilerParams(dimension_semantics=("arbitrary", "arbitrary"),
                                             vmem_limit_bytes=VMEM_LIMIT),
        name="attn",
    )(q, k, v, sza, bias)


def _attn_bias_table(rpb):
    n_dr = 2 * WIN_R - 1
    n_dc = 2 * WIN_C - 1
    qc = jnp.arange(GRID_W)
    kc = jnp.arange(GRID_W)
    col_start = jnp.clip(qc - WIN_C // 2, 0, GRID_W - WIN_C)
    ok = (kc[None, :] >= col_start[:, None]) & (kc[None, :] < col_start[:, None] + WIN_C)
    period = 2 * GRID_W
    lead = GRID_W - WIN_C
    ext = jnp.pad(rpb.astype(F32), ((0, 0), (0, 0), (lead, period - lead - n_dc)))
    flat = jnp.tile(ext, (1, 1, GRID_W))[:, :, :GRID_W * (period - 1)]
    band = flat.reshape(NA_HEADS, n_dr, GRID_W, period - 1)[:, :, :, GRID_W - 1:]
    band = jnp.where(ok[None, None], band, MASK_VALUE)
    band = band.reshape(NA_HEADS // 2, 2, n_dr, GRID_W, GRID_W)
    band = jnp.transpose(band, (0, 2, 1, 3, 4)).reshape(NA_HEADS // 2, n_dr, 2 * GRID_W, GRID_W)
    return jnp.concatenate([band[:, :-1], band[:, 1:]], axis=-1)


def _split3_dot(tri, x):
    x1 = x.astype(BF16)
    r1 = x - x1.astype(F32)
    x2 = r1.astype(BF16)
    x3 = (r1 - x2.astype(F32)).astype(BF16)
    return _dot(tri, x1) + _dot(tri, x2) + _dot(tri, x3)


def _dn_kernel(xf_ref, xb_ref, bgf_ref, bgb_ref, tl_ref, tu_ref, of_ref, ob_ref, s_ref):
    i = pl.program_id(1)
    tb = DN_BLOCK
    nc = tb // CHUNK

    @pl.when(i == 0)
    def _():
        s_ref[...] = jnp.zeros_like(s_ref)

    half = tb // 2
    ri = lax.broadcasted_iota(jnp.int32, (tb, half), 0) % half
    nn = lax.broadcasted_iota(jnp.int32, (tb, half), 1)
    same = (ri // CHUNK) == (nn // CHUNK)
    eye = (ri == nn).astype(F32)
    lane = lax.broadcasted_iota(jnp.int32, (tb, LANES), 1)
    is_g = (lane >= N_DIR * DN_HEADS) & (lane < 2 * N_DIR * DN_HEADS)

    def block_diag(x):
        z = jnp.zeros((half, half), x.dtype)
        return jnp.concatenate([jnp.concatenate([x[:half], z], axis=1),
                                jnp.concatenate([z, x[half:]], axis=1)], axis=0)

    def gram(a, b):
        return jnp.concatenate([_dot_nt(a[:half], b[:half]), _dot_nt(a[half:], b[half:])], axis=0)

    chains = []
    for bb, d in [(bb, d) for bb in range(DN_BATCH) for d in range(N_DIR)]:
        x_ref = xf_ref if d == 0 else xb_ref
        bg_ref = bgf_ref if d == 0 else bgb_ref
        incl = same & ((ri >= nn) if d == 0 else (ri <= nn))
        strict = same & ((ri > nn) if d == 0 else (ri < nn))

        bg = bg_ref[bb]
        gvals = jnp.where(is_g, bg, 0.0)
        gc = _split3_dot(tl_ref[...] if d == 0 else tu_ref[...], gvals)
        gct = gc.T
        eg = jnp.exp(gc)

        for h in range(DN_HEADS):
            bi = d * DN_HEADS + h
            ci = N_DIR * DN_HEADS + bi
            q_bf = x_ref[bb, :, h * DN_HEAD_DIM:(h + 1) * DN_HEAD_DIM]
            k_bf = x_ref[bb, :, DN_WIDTH + h * DN_HEAD_DIM:DN_WIDTH + (h + 1) * DN_HEAD_DIM]
            v_bf = x_ref[bb, :, 2 * DN_WIDTH + h * DN_HEAD_DIM:2 * DN_WIDTH + (h + 1) * DN_HEAD_DIM]
            kh = k_bf.astype(F32)
            beta = bg[:, bi:bi + 1]
            gcol_all = gc[:, ci:ci + 1]
            egcol = eg[:, ci:ci + 1]
            rhs = jnp.concatenate([kh * (beta * egcol), v_bf.astype(F32) * beta], axis=1).astype(BF16)
            kk = gram(k_bf, k_bf)
            qk = gram(q_bf, k_bf)
            diff = jnp.concatenate([gcol_all[:half] - gct[ci:ci + 1, :half],
                                    gcol_all[half:] - gct[ci:ci + 1, half:]], axis=0)
            decay = jnp.where(incl, jnp.exp(jnp.where(incl, diff, 0.0)), 0.0)
            glast, kdec = [], []
            for c in range(nc):
                rc = slice(c * CHUNK, (c + 1) * CHUNK)
                row = (c + 1) * CHUNK - 1 if d == 0 else c * CHUNK
                gl = gcol_all[row:row + 1]
                glast.append(gl)
                kdec.append((kh[rc] * jnp.exp(gl - gcol_all[rc])).astype(BF16))
            chains.append(dict(
                d=d, h=h, bb=bb, si=bb * N_DIR * DN_HEADS + bi,
                m=jnp.where(strict, -(kk * beta * decay), 0.0),
                intra=(qk * decay).astype(BF16),
                rhs=rhs, qdec=q_bf.astype(F32) * egcol, glast=glast, kdec=kdec))

    xs = [ch["m"].astype(BF16) for ch in chains]
    ps = [eye + ch["m"] for ch in chains]
    xs = [_dot(block_diag(x), x).astype(BF16) for x in xs]
    for _ in range(4):
        outs = [_dot(block_diag(x), jnp.concatenate([x, p.astype(BF16)], axis=1)) for x, p in zip(xs, ps)]
        xs = [o[:, :half].astype(BF16) for o in outs]
        ps = [p + o[:, half:] for p, o in zip(ps, outs)]
    ps = [p + _dot(block_diag(x), p.astype(BF16)) for x, p in zip(xs, ps)]

    for ch, p in zip(chains, ps):
        wu = _dot(block_diag(p.astype(BF16)), ch["rhs"]).astype(BF16)
        iw = _dot(block_diag(ch["intra"]), wu)
        ch["qp"] = (ch["qdec"] - iw[:, :DN_HEAD_DIM]).astype(BF16)
        ch["oi"] = iw[:, DN_HEAD_DIM:]
        ch["gh"] = [_dot_tn(ch["kdec"][c], wu[c * CHUNK:(c + 1) * CHUNK]) for c in range(nc)]
        ch["s"] = s_ref[ch["si"]]

    for step in range(nc):
        for ch in chains:
            c = step if ch["d"] == 0 else nc - 1 - step
            rc = slice(c * CHUNK, (c + 1) * CHUNK)
            o_ref = of_ref if ch["d"] == 0 else ob_ref
            s = ch["s"]
            sb = s.astype(BF16)
            o = _dot(ch["qp"][rc], sb) + ch["oi"][rc]
            hs = slice(ch["h"] * DN_HEAD_DIM, (ch["h"] + 1) * DN_HEAD_DIM)
            o_ref[ch["bb"], rc, hs] = o.astype(o_ref.dtype)
            gh = ch["gh"][c]
            ch["s"] = (s * jnp.exp(ch["glast"][c]) + gh[:, DN_HEAD_DIM:]
                       - _dot(gh[:, :DN_HEAD_DIM].astype(BF16), sb))

    for ch in chains:
        s_ref[ch["si"]] = ch["s"]


def _deltanet(qkvc, bg, tri_l, tri_u):
    b, t, _ = qkvc.shape
    tb = DN_BLOCK
    nblk = t // tb
    c3 = 3 * DN_WIDTH
    nb = DN_BATCH
    assert b % nb == 0
    fwd = lambda bi, i: (bi, i, 0)
    bwd = lambda bi, i: (bi, nblk - 1 - i, 0)
    return pl.pallas_call(
        _dn_kernel,
        grid=(b // nb, nblk),
        in_specs=[pl.BlockSpec((nb, tb, c3), fwd), pl.BlockSpec((nb, tb, c3), bwd),
                  pl.BlockSpec((nb, tb, LANES), fwd), pl.BlockSpec((nb, tb, LANES), bwd),
                  _const_spec((tb, tb)), _const_spec((tb, tb))],
        out_specs=[pl.BlockSpec((nb, tb, DN_WIDTH), fwd), pl.BlockSpec((nb, tb, DN_WIDTH), bwd)],
        out_shape=(jax.ShapeDtypeStruct((b, t, DN_WIDTH), BF16),
                   jax.ShapeDtypeStruct((b, t, DN_WIDTH), BF16)),
        scratch_shapes=[pltpu.VMEM((nb * N_DIR * DN_HEADS, DN_HEAD_DIM, DN_HEAD_DIM), F32)],
        compiler_params=pltpu.CompilerParams(dimension_semantics=("arbitrary", "arbitrary"),
                                             vmem_limit_bytes=VMEM_LIMIT),
        name="deltanet",
    )(qkvc, qkvc, bg, bg, tri_l, tri_u)


def _out_kernel(x_ref, oa_ref, of_ref, ob_ref, szb_ref, gate_ref, ng_ref, wa_ref, wb_ref, wo_ref, y_ref):
    o = of_ref[...].astype(F32) + ob_ref[...].astype(F32)
    parts = []
    for h in range(DN_HEADS):
        oh = o[:, h * DN_HEAD_DIM:(h + 1) * DN_HEAD_DIM]
        ms = jnp.mean(oh * oh, axis=-1, keepdims=True)
        parts.append(oh * lax.rsqrt(ms + EPS) * ng_ref[...])
    on = jnp.concatenate(parts, axis=1)
    o_b = (on * szb_ref[...].astype(F32)).astype(BF16)
    ya = _dot(oa_ref[...], wa_ref[...])
    yb = _dot(o_b, wb_ref[...])
    merged = (gate_ref[:, :D_MODEL].astype(F32) * ya + gate_ref[:, D_MODEL:].astype(F32) * yb).astype(BF16)
    y_ref[...] = x_ref[...] + _dot(merged, wo_ref[...])


def _out_stage(x2, oa, of, ob, szb, gates, ng, wa, wb, wo):
    n = x2.shape[0]
    tm = TM_OUT
    tok = lambda width: pl.BlockSpec((tm, width), lambda i: (i, 0))
    return pl.pallas_call(
        _out_kernel,
        grid=(n // tm,),
        in_specs=[tok(D_MODEL), tok(NA_WIDTH), tok(DN_WIDTH), tok(DN_WIDTH), tok(DN_WIDTH), tok(2 * D_MODEL),
                  _const_spec((1, DN_HEAD_DIM)), _const_spec((NA_WIDTH, D_MODEL)),
                  _const_spec((DN_WIDTH, D_MODEL)), _const_spec((D_MODEL, D_MODEL))],
        out_specs=tok(D_MODEL),
        out_shape=jax.ShapeDtypeStruct((n, D_MODEL), F32),
        compiler_params=pltpu.CompilerParams(dimension_semantics=("arbitrary",),
                                             vmem_limit_bytes=VMEM_LIMIT),
        name="out_stage",
    )(x2, oa, of, ob, szb, gates, ng, wa, wb, wo)


def _layer_params(l, norm_g, w_in, attn_q_norm_g, attn_k_norm_g, attn_rpb, dn_conv_w, dn_a_log, dn_dt_bias,
                  dn_norm_g, w_branch_a, w_branch_b, w_out):
    n_ba = 2 * N_DIR * DN_HEADS
    w = w_in[l]
    c_ba = 3 * NA_WIDTH + NA_WIDTH + 3 * DN_WIDTH + DN_WIDTH
    w_re = jnp.concatenate(
        [w[:, :c_ba], w[:, c_ba + n_ba:], w[:, c_ba:c_ba + n_ba],
         jnp.zeros((D_MODEL, LANES - n_ba), w.dtype)], axis=1).astype(BF16)
    pad = lambda vec, lo: jnp.zeros((1, LANES), F32).at[0, lo:lo + vec.size].set(vec.reshape(-1).astype(F32))
    head_id = jnp.arange(NA_WIDTH // 2) // NA_HEAD_DIM
    tok_id = jnp.arange(DN_BLOCK)
    same_chunk = (tok_id[:, None] // CHUNK) == (tok_id[None, :] // CHUNK)
    return dict(
        norm_g=norm_g[l].reshape(1, D_MODEL),
        w_re=w_re,
        gsum=(head_id[:, None] == head_id[None, :]).astype(BF16),
        qg=(jnp.tile(attn_q_norm_g[l], NA_HEADS) * NA_HEAD_DIM ** -0.5).reshape(1, NA_WIDTH),
        kg=jnp.tile(attn_k_norm_g[l], NA_HEADS).reshape(1, NA_WIDTH),
        alog=pad(dn_a_log[l], N_DIR * DN_HEADS),
        dt=pad(dn_dt_bias[l], N_DIR * DN_HEADS),
        bias=_attn_bias_table(attn_rpb[l]),
        conv_w8=jnp.concatenate([dn_conv_w[l].astype(F32), jnp.zeros((8 - CONV_K, 3 * DN_WIDTH), F32)], axis=0),
        tri_l=(same_chunk & (tok_id[:, None] >= tok_id[None, :])).astype(BF16),
        tri_u=(same_chunk & (tok_id[:, None] <= tok_id[None, :])).astype(BF16),
        dn_g=dn_norm_g[l].reshape(1, DN_HEAD_DIM),
        wa=w_branch_a[l].astype(BF16),
        wb=w_branch_b[l].astype(BF16),
        wo=w_out[l].astype(BF16),
    )


def _layer(x, p):
    b, t, _ = x.shape
    assert t % TM_PROJ == 0 and t % DN_BLOCK == 0 and (b * t) % TM_OUT == 0
    x2 = x.reshape(b * t, D_MODEL)
    q, k, v, sza, qkvc, szb, gates, bg = _in_proj(x2, t, p["norm_g"], p["w_re"], p["gsum"], p["qg"], p["kg"],
                                                  p["alog"], p["dt"], p["conv_w8"])
    r3 = lambda a: a.reshape(b, t, a.shape[-1])
    oa = _attention(r3(q), r3(k), r3(v), r3(sza), p["bias"])
    of, ob = _deltanet(r3(qkvc), r3(bg), p["tri_l"], p["tri_u"])
    y = _out_stage(x2, oa.reshape(b * t, NA_WIDTH), of.reshape(b * t, DN_WIDTH), ob.reshape(b * t, DN_WIDTH),
                   szb, gates, p["dn_g"], p["wa"], p["wb"], p["wo"])
    return y.reshape(b, t, D_MODEL)


def kernel(x_prompt, x_sample, norm_g, w_in, attn_q_norm_g, attn_k_norm_g, attn_rpb, dn_conv_w, dn_a_log,
           dn_dt_bias, dn_norm_g, w_branch_a, w_branch_b, w_out):
    depth = w_in.shape[0]
    params = [_layer_params(l, norm_g, w_in, attn_q_norm_g, attn_k_norm_g, attn_rpb, dn_conv_w, dn_a_log,
                            dn_dt_bias, dn_norm_g, w_branch_a, w_branch_b, w_out) for l in range(depth)]
    outs = []
    for x in (x_prompt, x_sample):
        for p in params:
            x = _layer(x, p)
        outs.append(x)
    return tuple(outs)
```

```python
import functools

import jax
import jax.numpy as jnp
from jax import lax
from jax.experimental import pallas as pl
from jax.experimental.pallas import tpu as pltpu

D_MODEL = 1024
GRID_W = 64
NA_HEADS = 8
NA_HEAD_DIM = 64
NA_WIDTH = NA_HEADS * NA_HEAD_DIM
WIN_R = 8
WIN_C = 16
DN_HEADS = 4
DN_HEAD_DIM = 128
DN_WIDTH = DN_HEADS * DN_HEAD_DIM
CONV_K = 5
CHUNK = 64
N_DIR = 2
EPS = 1e-6

LANES = 128
HALO = 16
MASK_VALUE = -1e30

C_QKV_A = 0
C_Z_A = 3 * NA_WIDTH
C_QKV_B = C_Z_A + NA_WIDTH
C_Z_B = C_QKV_B + 3 * DN_WIDTH
C_GATE = C_Z_B + DN_WIDTH
C_BG = C_GATE + 2 * D_MODEL
C_END = C_BG + LANES

TM_PROJ = 512
TM_OUT = 1024
ATTN_ROWS = 16
ATTN_UNROLL = 8
DN_BLOCK = 256
DN_BATCH = 2
VMEM_LIMIT = 48 * 1024 * 1024

F32 = jnp.float32
BF16 = jnp.bfloat16


def _const_spec(shape):
    return pl.BlockSpec(shape, lambda *_: (0,) * len(shape), pipeline_mode=pl.Buffered(1))


def _sigmoid(x):
    return 0.5 * jnp.tanh(0.5 * x) + 0.5


def _silu(x):
    h = 0.5 * x
    return h + h * jnp.tanh(h)


def _dot(a, b):
    return jnp.dot(a, b, preferred_element_type=F32)


def _dot_nt(a, b):
    return lax.dot_general(a, b, (((1,), (1,)), ((), ())), preferred_element_type=F32)


def _dot_tn(a, b):
    return lax.dot_general(a, b, (((0,), (0,)), ((), ())), preferred_element_type=F32)


def _in_proj_kernel(x_ref, xp_ref, xn_ref, g_ref, w_ref, gs_ref, qg_ref, kg_ref, alog_ref, dt_ref, cw_ref,
                    q_ref, k_ref, v_ref, sza_ref, qkvc_ref, szb_ref, gate_ref, bg_ref, pb_ref, *, tiles_per_seq):
    i = pl.program_id(0)
    tm = TM_PROJ
    x = jnp.concatenate([xp_ref[...], x_ref[...], xn_ref[...]], axis=0)
    ms = jnp.mean(x * x, axis=-1, keepdims=True)
    h_ext = (x * lax.rsqrt(ms + EPS) * g_ref[...]).astype(BF16)
    h = h_ext[HALO:HALO + tm]
    gs = gs_ref[...]

    def proj(lo, hi):
        return _dot(h, w_ref[:, lo:hi])

    def head_norm(y, gain):
        y2 = (y * y).astype(BF16)
        half = NA_WIDTH // 2
        ss = jnp.concatenate([_dot(y2[:, :half], gs), _dot(y2[:, half:], gs)], axis=1)
        return y * lax.rsqrt(ss * (1.0 / NA_HEAD_DIM) + EPS) * gain

    row = lax.broadcasted_iota(jnp.int32, (tm + 2 * HALO, 1), 0)
    seq_tile = i % tiles_per_seq
    valid = ((row >= HALO) | (seq_tile > 0)) & ((row < HALO + tm) | (seq_tile < tiles_per_seq - 1))
    def ordering_zero(y):
        bits = pltpu.bitcast(y[0:8, 0:LANES], jnp.uint32)
        bits = lax.shift_right_logical(lax.shift_right_logical(bits, jnp.uint32(16)), jnp.uint32(16))
        return pltpu.bitcast(bits, F32)[0:1, :]

    def conv_slab(slab, zero_row):
        cols = slice(slab * DN_HEAD_DIM, (slab + 1) * DN_HEAD_DIM)
        acc = zero_row
        for j in range(CONV_K):
            start = HALO - CONV_K // 2 + j
            acc = acc + pb_ref[start:start + tm, cols] * cw_ref[j:j + 1, cols]
        y = _silu(acc)
        if slab < 2 * DN_HEADS:
            y = y * lax.rsqrt(jnp.sum(y * y, axis=-1, keepdims=True) + EPS)
        if slab < DN_HEADS:
            y = y * (DN_HEAD_DIM ** -0.5)
        qkvc_ref[:, cols] = y.astype(BF16)

    slabs = iter(range(3 * DN_HEADS))

    def conv_some(count, after):
        zero_row = ordering_zero(after)
        for _ in range(count):
            conv_slab(next(slabs), zero_row)

    for part in range(3):
        cols = slice(part * DN_WIDTH, (part + 1) * DN_WIDTH)
        lo = C_QKV_B + part * DN_WIDTH
        pr = _dot(h_ext, w_ref[:, lo:lo + DN_WIDTH])
        pb_ref[:, cols] = jnp.where(valid, pr, 0.0)
        if part > 0:
            conv_some(1, pr)
    pr = proj(C_QKV_A, C_QKV_A + NA_WIDTH)
    q_ref[...] = head_norm(pr, qg_ref[...]).astype(BF16)
    conv_some(1, pr)
    pr = proj(C_QKV_A + NA_WIDTH, C_QKV_A + 2 * NA_WIDTH)
    k_ref[...] = head_norm(pr, kg_ref[...]).astype(BF16)
    conv_some(1, pr)
    pr = proj(C_QKV_A + 2 * NA_WIDTH, C_Z_A)
    v_ref[...] = pr.astype(BF16)
    conv_some(1, pr)
    pr = proj(C_Z_A, C_QKV_B)
    sza_ref[...] = _silu(pr).astype(BF16)
    conv_some(1, pr)
    pr = proj(C_Z_B, C_GATE)
    szb_ref[...] = _silu(pr).astype(BF16)
    conv_some(1, pr)
    for part in range(4):
        lo = C_GATE + part * DN_WIDTH
        pr = proj(lo, lo + DN_WIDTH)
        gate_ref[:, part * DN_WIDTH:(part + 1) * DN_WIDTH] = _sigmoid(pr).astype(BF16)
        conv_some(1, pr)
    ba = proj(C_BG, C_END)
    lane = lax.broadcasted_iota(jnp.int32, ba.shape, 1)
    z = ba + dt_ref[...]
    sp = jnp.maximum(z, 0.0) + jnp.log1p(jnp.exp(-jnp.abs(z)))
    g = -jnp.exp(alog_ref[...]) * sp
    is_g = (lane >= N_DIR * DN_HEADS) & (lane < 2 * N_DIR * DN_HEADS)
    bg_ref[...] = jnp.where(lane < N_DIR * DN_HEADS, _sigmoid(ba), jnp.where(is_g, g, 0.0))
    conv_some(1, ba)


def _in_proj(x2, seq_len, norm_g, w_re, gsum, qg, kg, alog_vec, dt_vec, conv_w8):
    n = x2.shape[0]
    tm = TM_PROJ
    hpt = tm // HALO
    n_halo = n // HALO
    tok = lambda width: pl.BlockSpec((tm, width), lambda i: (i, 0))
    out_shapes = (
        jax.ShapeDtypeStruct((n, NA_WIDTH), BF16),
        jax.ShapeDtypeStruct((n, NA_WIDTH), BF16),
        jax.ShapeDtypeStruct((n, NA_WIDTH), BF16),
        jax.ShapeDtypeStruct((n, NA_WIDTH), BF16),
        jax.ShapeDtypeStruct((n, 3 * DN_WIDTH), BF16),
        jax.ShapeDtypeStruct((n, DN_WIDTH), BF16),
        jax.ShapeDtypeStruct((n, 2 * D_MODEL), BF16),
        jax.ShapeDtypeStruct((n, LANES), F32),
    )
    return pl.pallas_call(
        functools.partial(_in_proj_kernel, tiles_per_seq=seq_len // tm),
        grid=(n // tm,),
        in_specs=[tok(D_MODEL),
                  pl.BlockSpec((HALO, D_MODEL), lambda i: (jnp.maximum(i * hpt - 1, 0), 0)),
                  pl.BlockSpec((HALO, D_MODEL), lambda i: (jnp.minimum((i + 1) * hpt, n_halo - 1), 0)),
                  _const_spec((1, D_MODEL)), _const_spec((D_MODEL, C_END)),
                  _const_spec((NA_WIDTH // 2, NA_WIDTH // 2)), _const_spec((1, NA_WIDTH)),
                  _const_spec((1, NA_WIDTH)),
                  _const_spec((1, LANES)), _const_spec((1, LANES)), _const_spec((8, 3 * DN_WIDTH))],
        out_specs=[tok(NA_WIDTH), tok(NA_WIDTH), tok(NA_WIDTH), tok(NA_WIDTH), tok(3 * DN_WIDTH),
                   tok(DN_WIDTH), tok(2 * D_MODEL), tok(LANES)],
        out_shape=out_shapes,
        scratch_shapes=[pltpu.VMEM((tm + 2 * HALO, 3 * DN_WIDTH), F32)],
        compiler_params=pltpu.CompilerParams(dimension_semantics=("arbitrary",),
                                             vmem_limit_bytes=VMEM_LIMIT),
        name="in_proj",
    )(x2, x2, x2, norm_g, w_re, gsum, qg, kg, alog_vec, dt_vec, conv_w8)


def _attn_kernel(q_ref, k_ref, v_ref, sza_ref, bias_ref, o_ref, *, rows):
    blk = pl.program_id(1)
    lane = lax.broadcasted_iota(jnp.int32, (GRID_W, LANES), 1)
    lo_half = lane < NA_HEAD_DIM
    n_keys = WIN_R * GRID_W
    n_pairs = NA_HEADS // 2
    ones = jnp.ones((n_keys, LANES), BF16)

    def row_body(rl, carry):
        r = blk * ATTN_ROWS + rl
        rs = jnp.clip(r - WIN_R // 2, 0, rows - WIN_R)
        var = r - rs
        kstart = pl.multiple_of(rs * GRID_W, GRID_W)
        qstart = pl.multiple_of(rl * GRID_W, GRID_W)
        st = []
        for p in range(n_pairs):
            cols = slice(p * LANES, (p + 1) * LANES)
            q2 = q_ref[0, pl.ds(qstart, GRID_W), cols]
            zero = jnp.zeros_like(q2)
            qm = jnp.concatenate([jnp.where(lo_half, q2, zero), jnp.where(lo_half, zero, q2)], axis=0)
            k2 = k_ref[0, pl.ds(kstart, n_keys), cols]
            sd = _dot_nt(qm, k2)
            st.append(jnp.concatenate(
                [sd[:, j * LANES:(j + 1) * LANES] + bias_ref[p, 2 * j + WIN_R - 1 - var] for j in range(WIN_R // 2)],
                axis=1))
        mx = [jnp.max(s, axis=-1, keepdims=True) for s in st]
        pe = [jnp.exp(s - m).astype(BF16) for s, m in zip(st, mx)]
        for p in range(n_pairs):
            cols = slice(p * LANES, (p + 1) * LANES)
            v2 = v_ref[0, pl.ds(kstart, n_keys), cols]
            ox = _dot(pe[p], jnp.concatenate([v2, ones], axis=1))
            on = ox[:, :LANES] / ox[:, LANES:]
            o2 = jnp.where(lo_half, on[:GRID_W], on[GRID_W:])
            sz = sza_ref[0, pl.ds(qstart, GRID_W), cols].astype(F32)
            o_ref[0, pl.ds(qstart, GRID_W), cols] = (o2 * sz).astype(BF16)
        return carry

    lax.fori_loop(0, ATTN_ROWS, row_body, 0, unroll=ATTN_UNROLL)


def _attention(q, k, v, sza, bias):
    b, t, _ = q.shape
    rows = t // GRID_W
    assert rows >= WIN_R and rows % ATTN_ROWS == 0
    tq = ATTN_ROWS * GRID_W
    blk = pl.BlockSpec((1, tq, NA_WIDTH), lambda bi, i: (bi, i, 0))
    full = pl.BlockSpec((1, t, NA_WIDTH), lambda bi, i: (bi, 0, 0), pipeline_mode=pl.Buffered(1))
    return pl.pallas_call(
        functools.partial(_attn_kernel, rows=rows),
        grid=(b, rows // ATTN_ROWS),
        in_specs=[blk, full, full, blk, _const_spec(bias.shape)],
        out_specs=blk,
        out_shape=jax.ShapeDtypeStruct((b, t, NA_WIDTH), BF16),
        compiler_params=pltpu.CompilerParams(dimension_semantics=("arbitrary", "arbitrary"),
                                             vmem_limit_bytes=VMEM_LIMIT),
        name="attn",
    )(q, k, v, sza, bias)


def _attn_bias_table(rpb):
    n_dr = 2 * WIN_R - 1
    n_dc = 2 * WIN_C - 1
    qc = jnp.arange(GRID_W)
    kc = jnp.arange(GRID_W)
    col_start = jnp.clip(qc - WIN_C // 2, 0, GRID_W - WIN_C)
    ok = (kc[None, :] >= col_start[:, None]) & (kc[None, :] < col_start[:, None] + WIN_C)
    period = 2 * GRID_W
    lead = GRID_W - WIN_C
    ext = jnp.pad(rpb.astype(F32), ((0, 0), (0, 0), (lead, period - lead - n_dc)))
    flat = jnp.tile(ext, (1, 1, GRID_W))[:, :, :GRID_W * (period - 1)]
    band = flat.reshape(NA_HEADS, n_dr, GRID_W, period - 1)[:, :, :, GRID_W - 1:]
    band = jnp.where(ok[None, None], band, MASK_VALUE)
    band = band.reshape(NA_HEADS // 2, 2, n_dr, GRID_W, GRID_W)
    band = jnp.transpose(band, (0, 2, 1, 3, 4)).reshape(NA_HEADS // 2, n_dr, 2 * GRID_W, GRID_W)
    return jnp.concatenate([band[:, :-1], band[:, 1:]], axis=-1)


def _split3_dot(tri, x):
    x1 = x.astype(BF16)
    r1 = x - x1.astype(F32)
    x2 = r1.astype(BF16)
    x3 = (r1 - x2.astype(F32)).astype(BF16)
    return _dot(tri, x1) + _dot(tri, x2) + _dot(tri, x3)


def _dn_kernel(xf_ref, xb_ref, bgf_ref, bgb_ref, tl_ref, tu_ref, of_ref, ob_ref, s_ref):
    i = pl.program_id(1)
    tb = DN_BLOCK
    nc = tb // CHUNK

    @pl.when(i == 0)
    def _():
        s_ref[...] = jnp.zeros_like(s_ref)

    half = tb // 2
    ri = lax.broadcasted_iota(jnp.int32, (tb, half), 0) % half
    nn = lax.broadcasted_iota(jnp.int32, (tb, half), 1)
    same = (ri // CHUNK) == (nn // CHUNK)
    eye = (ri == nn).astype(F32)
    lane = lax.broadcasted_iota(jnp.int32, (tb, LANES), 1)
    is_g = (lane >= N_DIR * DN_HEADS) & (lane < 2 * N_DIR * DN_HEADS)

    def block_diag(x):
        z = jnp.zeros((half, half), x.dtype)
        return jnp.concatenate([jnp.concatenate([x[:half], z], axis=1),
                                jnp.concatenate([z, x[half:]], axis=1)], axis=0)

    def gram(a, b):
        return jnp.concatenate([_dot_nt(a[:half], b[:half]), _dot_nt(a[half:], b[half:])], axis=0)

    chains = []
    for bb, d in [(bb, d) for bb in range(DN_BATCH) for d in range(N_DIR)]:
        x_ref = xf_ref if d == 0 else xb_ref
        bg_ref = bgf_ref if d == 0 else bgb_ref
        incl = same & ((ri >= nn) if d == 0 else (ri <= nn))
        strict = same & ((ri > nn) if d == 0 else (ri < nn))

        bg = bg_ref[bb]
        gvals = jnp.where(is_g, bg, 0.0)
        gc = _split3_dot(tl_ref[...] if d == 0 else tu_ref[...], gvals)
        gct = gc.T
        eg = jnp.exp(gc)

        for h in range(DN_HEADS):
            bi = d * DN_HEADS + h
            ci = N_DIR * DN_HEADS + bi
            q_bf = x_ref[bb, :, h * DN_HEAD_DIM:(h + 1) * DN_HEAD_DIM]
            k_bf = x_ref[bb, :, DN_WIDTH + h * DN_HEAD_DIM:DN_WIDTH + (h + 1) * DN_HEAD_DIM]
            v_bf = x_ref[bb, :, 2 * DN_WIDTH + h * DN_HEAD_DIM:2 * DN_WIDTH + (h + 1) * DN_HEAD_DIM]
            kh = k_bf.astype(F32)
            beta = bg[:, bi:bi + 1]
            gcol_all = gc[:, ci:ci + 1]
            egcol = eg[:, ci:ci + 1]
            rhs = jnp.concatenate([kh * (beta * egcol), v_bf.astype(F32) * beta], axis=1).astype(BF16)
            kk = gram(k_bf, k_bf)
            qk = gram(q_bf, k_bf)
            diff = jnp.concatenate([gcol_all[:half] - gct[ci:ci + 1, :half],
                                    gcol_all[half:] - gct[ci:ci + 1, half:]], axis=0)
            decay = jnp.where(incl, jnp.exp(jnp.where(incl, diff, 0.0)), 0.0)
            glast, kdec = [], []
            for c in range(nc):
                rc = slice(c * CHUNK, (c + 1) * CHUNK)
                row = (c + 1) * CHUNK - 1 if d == 0 else c * CHUNK
                gl = gcol_all[row:row + 1]
                glast.append(gl)
                kdec.append((kh[rc] * jnp.exp(gl - gcol_all[rc])).astype(BF16))
            chains.append(dict(
                d=d, h=h, bb=bb, si=bb * N_DIR * DN_HEADS + bi,
                m=jnp.where(strict, -(kk * beta * decay), 0.0),
                intra=(qk * decay).astype(BF16),
                rhs=rhs, qdec=q_bf.astype(F32) * egcol, glast=glast, kdec=kdec))

    xs = [ch["m"].astype(BF16) for ch in chains]
    ps = [eye + ch["m"] for ch in chains]
    xs = [_dot(block_diag(x), x).astype(BF16) for x in xs]
    for _ in range(4):
        outs = [_dot(block_diag(x), jnp.concatenate([x, p.astype(BF16)], axis=1)) for x, p in zip(xs, ps)]
        xs = [o[:, :half].astype(BF16) for o in outs]
        ps = [p + o[:, half:] for p, o in zip(ps, outs)]
    ps = [p + _dot(block_diag(x), p.astype(BF16)) for x, p in zip(xs, ps)]

    for ch, p in zip(chains, ps):
        wu = _dot(block_diag(p.astype(BF16)), ch["rhs"]).astype(BF16)
        iw = _dot(block_diag(ch["intra"]), wu)
        ch["qp"] = (ch["qdec"] - iw[:, :DN_HEAD_DIM]).astype(BF16)
        ch["oi"] = iw[:, DN_HEAD_DIM:]
        ch["gh"] = [_dot_tn(ch["kdec"][c], wu[c * CHUNK:(c + 1) * CHUNK]) for c in range(nc)]
        ch["s"] = s_ref[ch["si"]]

    for step in range(nc):
        for ch in chains:
            c = step if ch["d"] == 0 else nc - 1 - step
            rc = slice(c * CHUNK, (c + 1) * CHUNK)
            o_ref = of_ref if ch["d"] == 0 else ob_ref
            s = ch["s"]
            sb = s.astype(BF16)
            o = _dot(ch["qp"][rc], sb) + ch["oi"][rc]
            hs = slice(ch["h"] * DN_HEAD_DIM, (ch["h"] + 1) * DN_HEAD_DIM)
            o_ref[ch["bb"], rc, hs] = o.astype(o_ref.dtype)
            gh = ch["gh"][c]
            ch["s"] = (s * jnp.exp(ch["glast"][c]) + gh[:, DN_HEAD_DIM:]
                       - _dot(gh[:, :DN_HEAD_DIM].astype(BF16), sb))

    for ch in chains:
        s_ref[ch["si"]] = ch["s"]


def _deltanet(qkvc, bg, tri_l, tri_u):
    b, t, _ = qkvc.shape
    tb = DN_BLOCK
    nblk = t // tb
    c3 = 3 * DN_WIDTH
    nb = DN_BATCH
    assert b % nb == 0
    fwd = lambda bi, i: (bi, i, 0)
    bwd = lambda bi, i: (bi, nblk - 1 - i, 0)
    return pl.pallas_call(
        _dn_kernel,
        grid=(b // nb, nblk),
        in_specs=[pl.BlockSpec((nb, tb, c3), fwd), pl.BlockSpec((nb, tb, c3), bwd),
                  pl.BlockSpec((nb, tb, LANES), fwd), pl.BlockSpec((nb, tb, LANES), bwd),
                  _const_spec((tb, tb)), _const_spec((tb, tb))],
        out_specs=[pl.BlockSpec((nb, tb, DN_WIDTH), fwd), pl.BlockSpec((nb, tb, DN_WIDTH), bwd)],
        out_shape=(jax.ShapeDtypeStruct((b, t, DN_WIDTH), BF16),
                   jax.ShapeDtypeStruct((b, t, DN_WIDTH), BF16)),
        scratch_shapes=[pltpu.VMEM((nb * N_DIR * DN_HEADS, DN_HEAD_DIM, DN_HEAD_DIM), F32)],
        compiler_params=pltpu.CompilerParams(dimension_semantics=("arbitrary", "arbitrary"),
                                             vmem_limit_bytes=VMEM_LIMIT),
        name="deltanet",
    )(qkvc, qkvc, bg, bg, tri_l, tri_u)


def _out_kernel(x_ref, oa_ref, of_ref, ob_ref, szb_ref, gate_ref, ng_ref, wa_ref, wb_ref, wo_ref, y_ref):
    o = of_ref[...].astype(F32) + ob_ref[...].astype(F32)
    parts = []
    for h in range(DN_HEADS):
        oh = o[:, h * DN_HEAD_DIM:(h + 1) * DN_HEAD_DIM]
        ms = jnp.mean(oh * oh, axis=-1, keepdims=True)
        parts.append(oh * lax.rsqrt(ms + EPS) * ng_ref[...])
    on = jnp.concatenate(parts, axis=1)
    o_b = (on * szb_ref[...].astype(F32)).astype(BF16)
    ya = _dot(oa_ref[...], wa_ref[...])
    yb = _dot(o_b, wb_ref[...])
    merged = (gate_ref[:, :D_MODEL].astype(F32) * ya + gate_ref[:, D_MODEL:].astype(F32) * yb).astype(BF16)
    y_ref[...] = x_ref[...] + _dot(merged, wo_ref[...])


def _out_stage(x2, oa, of, ob, szb, gates, ng, wa, wb, wo):
    n = x2.shape[0]
    tm = TM_OUT
    tok = lambda width: pl.BlockSpec((tm, width), lambda i: (i, 0))
    return pl.pallas_call(
        _out_kernel,
        grid=(n // tm,),
        in_specs=[tok(D_MODEL), tok(NA_WIDTH), tok(DN_WIDTH), tok(DN_WIDTH), tok(DN_WIDTH), tok(2 * D_MODEL),
                  _const_spec((1, DN_HEAD_DIM)), _const_spec((NA_WIDTH, D_MODEL)),
                  _const_spec((DN_WIDTH, D_MODEL)), _const_spec((D_MODEL, D_MODEL))],
        out_specs=tok(D_MODEL),
        out_shape=jax.ShapeDtypeStruct((n, D_MODEL), F32),
        compiler_params=pltpu.CompilerParams(dimension_semantics=("arbitrary",),
                                             vmem_limit_bytes=VMEM_LIMIT),
        name="out_stage",
    )(x2, oa, of, ob, szb, gates, ng, wa, wb, wo)


def _layer_params(l, norm_g, w_in, attn_q_norm_g, attn_k_norm_g, attn_rpb, dn_conv_w, dn_a_log, dn_dt_bias,
                  dn_norm_g, w_branch_a, w_branch_b, w_out):
    n_ba = 2 * N_DIR * DN_HEADS
    w = w_in[l]
    c_ba = 3 * NA_WIDTH + NA_WIDTH + 3 * DN_WIDTH + DN_WIDTH
    w_re = jnp.concatenate(
        [w[:, :c_ba], w[:, c_ba + n_ba:], w[:, c_ba:c_ba + n_ba],
         jnp.zeros((D_MODEL, LANES - n_ba), w.dtype)], axis=1).astype(BF16)
    pad = lambda vec, lo: jnp.zeros((1, LANES), F32).at[0, lo:lo + vec.size].set(vec.reshape(-1).astype(F32))
    head_id = jnp.arange(NA_WIDTH // 2) // NA_HEAD_DIM
    tok_id = jnp.arange(DN_BLOCK)
    same_chunk = (tok_id[:, None] // CHUNK) == (tok_id[None, :] // CHUNK)
    return dict(
        norm_g=norm_g[l].reshape(1, D_MODEL),
        w_re=w_re,
        gsum=(head_id[:, None] == head_id[None, :]).astype(BF16),
        qg=(jnp.tile(attn_q_norm_g[l], NA_HEADS) * NA_HEAD_DIM ** -0.5).reshape(1, NA_WIDTH),
        kg=jnp.tile(attn_k_norm_g[l], NA_HEADS).reshape(1, NA_WIDTH),
        alog=pad(dn_a_log[l], N_DIR * DN_HEADS),
        dt=pad(dn_dt_bias[l], N_DIR * DN_HEADS),
        bias=_attn_bias_table(attn_rpb[l]),
        conv_w8=jnp.concatenate([dn_conv_w[l].astype(F32), jnp.zeros((8 - CONV_K, 3 * DN_WIDTH), F32)], axis=0),
        tri_l=(same_chunk & (tok_id[:, None] >= tok_id[None, :])).astype(BF16),
        tri_u=(same_chunk & (tok_id[:, None] <= tok_id[None, :])).astype(BF16),
        dn_g=dn_norm_g[l].reshape(1, DN_HEAD_DIM),
        wa=w_branch_a[l].astype(BF16),
        wb=w_branch_b[l].astype(BF16),
        wo=w_out[l].astype(BF16),
    )


def _layer(x, p):
    b, t, _ = x.shape
    assert t % TM_PROJ == 0 and t % DN_BLOCK == 0 and (b * t) % TM_OUT == 0
    x2 = x.reshape(b * t, D_MODEL)
    q, k, v, sza, qkvc, szb, gates, bg = _in_proj(x2, t, p["norm_g"], p["w_re"], p["gsum"], p["qg"], p["kg"],
                                                  p["alog"], p["dt"], p["conv_w8"])
    r3 = lambda a: a.reshape(b, t, a.shape[-1])
    oa = _attention(r3(q), r3(k), r3(v), r3(sza), p["bias"])
    of, ob = _deltanet(r3(qkvc), r3(bg), p["tri_l"], p["tri_u"])
    y = _out_stage(x2, oa.reshape(b * t, NA_WIDTH), of.reshape(b * t, DN_WIDTH), ob.reshape(b * t, DN_WIDTH),
                   szb, gates, p["dn_g"], p["wa"], p["wb"], p["wo"])
    return y.reshape(b, t, D_MODEL)


def kernel(x_prompt, x_sample, norm_g, w_in, attn_q_norm_g, attn_k_norm_g, attn_rpb, dn_conv_w, dn_a_log,
           dn_dt_bias, dn_norm_g, w_branch_a, w_branch_b, w_out):
    depth = w_in.shape[0]
    params = [_layer_params(l, norm_g, w_in, attn_q_norm_g, attn_k_norm_g, attn_rpb, dn_conv_w, dn_a_log,
                            dn_dt_bias, dn_norm_g, w_branch_a, w_branch_b, w_out) for l in range(depth)]
    outs = []
    for x in (x_prompt, x_sample):
        for p in params:
            x = _layer(x, p)
        outs.append(x)
    return tuple(outs)
```
